```python
import math
import jax
import jax.numpy as jnp
from jax import lax
import numpy as np

D_MODEL = 1024
BATCH = 2
SEQ = 8192
DEPTH = 4

S5_WIDTH = D_MODEL // 2
S5_GROUP = 16
S5_GROUPS = S5_WIDTH // S5_GROUP
S5_STATE = 64
DT_MIN = 1e-3
DT_MAX = 1e-1
ML_WIDTH = D_MODEL
ML_HEADS = 4
ML_HEAD_DIM = ML_WIDTH // ML_HEADS
ML_CONV = 4
ML_CHUNK = 128
D_FF = 11 * D_MODEL // 4
NORM_EPS = 1e-6
IN_COLS = S5_WIDTH + 3 * ML_WIDTH + 2 * ML_HEADS + 2 * D_MODEL

kernel_name = 'hybrid_s5_mlstm_macaron'


def rmsnorm(x, g):
    xf = x.astype(jnp.float32)
    y = xf * lax.rsqrt(jnp.mean(xf * xf, axis=-1, keepdims=True) + NORM_EPS)
    return (y * g.astype(jnp.float32)).astype(x.dtype)


def swiglu(h, wg, wu, wd):
    return (jax.nn.silu(h @ wg) * (h @ wu)) @ wd


def _complex_affine_combine(e1, e2):
    a1r, a1i, b1r, b1i = e1
    a2r, a2i, b2r, b2i = e2
    return (a2r * a1r - a2i * a1i,
            a2r * a1i + a2i * a1r,
            a2r * b1r - a2i * b1i + b2r,
            a2r * b1i + a2i * b1r + b2i)


def s5_branch(u, lam_re, lam_im, log_dt, b_re, b_im, c_re, c_im, d_skip, glu_v, glu_g):
    f32 = jnp.float32
    bsz, L, _ = u.shape
    uf = u.astype(f32).reshape(bsz, L, S5_GROUPS, S5_GROUP)
    dt = jnp.exp(log_dt.astype(f32))[:, None]
    lr = jnp.minimum(lam_re.astype(f32), -1e-4)
    li = lam_im.astype(f32)
    mag = jnp.exp(lr * dt)
    ab_re = mag * jnp.cos(li * dt)
    ab_im = mag * jnp.sin(li * dt)
    den = lr * lr + li * li
    q_re = ((ab_re - 1.0) * lr + ab_im * li) / den
    q_im = (ab_im * lr - (ab_re - 1.0) * li) / den
    br = b_re.astype(f32)
    bi = b_im.astype(f32)
    bb_re = q_re[..., None] * br - q_im[..., None] * bi
    bb_im = q_re[..., None] * bi + q_im[..., None] * br
    bu_re = jnp.einsum('blgp,gnp->blgn', uf, bb_re)
    bu_im = jnp.einsum('blgp,gnp->blgn', uf, bb_im)
    a_re = jnp.broadcast_to(ab_re, bu_re.shape)
    a_im = jnp.broadcast_to(ab_im, bu_im.shape)
    _, _, s_re, s_im = lax.associative_scan(_complex_affine_combine, (a_re, a_im, bu_re, bu_im), axis=1)
    y = (jnp.einsum('blgn,gpn->blgp', s_re, c_re.astype(f32))
         - jnp.einsum('blgn,gpn->blgp', s_im, c_im.astype(f32)))
    y = y.reshape(bsz, L, S5_WIDTH) + d_skip.astype(f32) * u.astype(f32)
    z = jax.nn.gelu(y).astype(u.dtype)
    return (z @ glu_v) * jax.nn.sigmoid(z @ glu_g)


def causal_conv(x, w, b):
    K = w.shape[0]
    L = x.shape[1]
    xp = jnp.pad(x, ((0, 0), (K - 1, 0), (0, 0)))
    y = b
    for j in range(K):
        y = y + w[j] * xp[:, j:j + L]
    return y


def mlstm_chunkwise(q, k, v, i_pre, f_pre):
    bsz, H, L, d = q.shape
    nc = L // ML_CHUNK

    def to_chunks(t):
        return jnp.moveaxis(t.reshape((bsz, H, nc, ML_CHUNK) + t.shape[3:]), 2, 0)

    qc, kc, vc = to_chunks(q), to_chunks(k), to_chunks(v)
    ic = to_chunks(i_pre)
    lfc = to_chunks(jax.nn.log_sigmoid(f_pre))
    causal = jnp.tril(jnp.ones((ML_CHUNK, ML_CHUNK), dtype=bool))

    def step(carry, inp):
        C, n, m = carry
        qb, kb, vb, ib, lfb = inp
        b = jnp.cumsum(lfb, axis=-1)
        dmat = b[..., :, None] - b[..., None, :] + ib[..., None, :]
        dmat = jnp.where(causal, dmat, -jnp.inf)
        inter = b + m[..., None]
        m_t = jnp.maximum(inter, jnp.max(dmat, axis=-1))
        w_inter = jnp.exp(inter - m_t)
        s = jnp.einsum('bhtd,bhsd->bhts', qb, kb) * jnp.exp(dmat - m_t[..., None])
        num = (w_inter[..., None] * jnp.einsum('bhtd,bhde->bhte', qb, C)
               + jnp.einsum('bhts,bhse->bhte', s, vb))
        den = w_inter * jnp.einsum('bhtd,bhd->bht', qb, n) + jnp.sum(s, axis=-1)
        h = num / jnp.maximum(jnp.abs(den), jnp.exp(-m_t))[..., None]
        g = b[..., -1]
        decay = g[..., None] - b + ib
        m_new = jnp.maximum(g + m, jnp.max(decay, axis=-1))
        w_old = jnp.exp(g + m - m_new)
        kw = kb * jnp.exp(decay - m_new[..., None])[..., None]
        C_new = w_old[..., None, None] * C + jnp.einsum('bhsd,bhse->bhde', kw, vb)
        n_new = w_old[..., None] * n + jnp.sum(kw, axis=2)
        return (C_new, n_new, m_new), h

    init = (jnp.zeros((bsz, H, d, d), jnp.float32),
            jnp.zeros((bsz, H, d), jnp.float32),
            jnp.zeros((bsz, H), jnp.float32))
    _, hs = lax.scan(step, init, (qc, kc, vc, ic, lfc))
    return jnp.moveaxis(hs, 0, 2).reshape(bsz, H, L, d)


def mlstm_branch(x_ml, v, o, if_pre, conv_w, conv_b, w_q, w_k, norm_g, skip):
    f32 = jnp.float32
    bsz, L, _ = x_ml.shape
    xc = jax.nn.silu(causal_conv(x_ml, conv_w, conv_b))
    xh = xc.reshape(bsz, L, ML_HEADS, ML_HEAD_DIM)
    q = jnp.einsum('blhd,hde->bhle', xh, w_q).astype(f32) * (ML_HEAD_DIM ** -0.5)
    k = jnp.einsum('blhd,hde->bhle', xh, w_k).astype(f32)
    vh = v.reshape(bsz, L, ML_HEADS, ML_HEAD_DIM).transpose(0, 2, 1, 3).astype(f32)
    gif = if_pre.astype(f32)
    i_pre = gif[..., :ML_HEADS].transpose(0, 2, 1)
    f_pre = gif[..., ML_HEADS:].transpose(0, 2, 1)
    h_tilde = mlstm_chunkwise(q, k, vh, i_pre, f_pre).transpose(0, 2, 1, 3)
    og = jax.nn.sigmoid(o.astype(f32)).reshape(bsz, L, ML_HEADS, ML_HEAD_DIM)
    hc = og * h_tilde
    hn = hc * lax.rsqrt(jnp.mean(hc * hc, axis=-1, keepdims=True) + NORM_EPS)
    out = hn.reshape(bsz, L, ML_WIDTH) * norm_g.astype(f32) + skip.astype(f32) * xc.astype(f32)
    return out.astype(x_ml.dtype)


def hybrid_mixer(h, w_in, b_if, lam_re, lam_im, log_dt, b_re, b_im, c_re, c_im, d_skip, glu_v, glu_g,
                 conv_w, conv_b, w_q, w_k, ml_norm, ml_skip, w_br_s5, w_br_ml, w_out):
    o0 = S5_WIDTH
    o1 = o0 + ML_WIDTH
    o2 = o1 + ML_WIDTH
    o3 = o2 + ML_WIDTH
    o4 = o3 + 2 * ML_HEADS
    proj = h @ w_in
    y_s5 = s5_branch(proj[..., :o0], lam_re, lam_im, log_dt, b_re, b_im, c_re, c_im, d_skip, glu_v, glu_g)
    y_ml = mlstm_branch(proj[..., o0:o1], proj[..., o1:o2], proj[..., o2:o3], proj[..., o3:o4] + b_if,
                        conv_w, conv_b, w_q, w_k, ml_norm, ml_skip)
    gates = jax.nn.sigmoid(proj[..., o4:].astype(jnp.float32))
    mix = (gates[..., :D_MODEL] * (y_s5 @ w_br_s5).astype(jnp.float32)
           + gates[..., D_MODEL:] * (y_ml @ w_br_ml).astype(jnp.float32))
    return mix.astype(h.dtype) @ w_out


def setup_inputs(seed: int = 0) -> dict:
    key = jax.random.key(seed)
    ks = iter(jax.random.split(key, 48))
    f32 = jnp.float32

    def nrm(shape, scale):
        return scale * jax.random.normal(next(ks), shape, f32)

    def gain(shape):
        return 1.0 + nrm(shape, 0.02)

    res_scale = (2.0 * DEPTH) ** -0.5
    n_idx = jnp.arange(S5_STATE, dtype=f32)
    x = nrm((BATCH, SEQ, D_MODEL), 1.0)
    ffn1_norm = gain((DEPTH, D_MODEL))
    ffn1_wg = nrm((DEPTH, D_MODEL, D_FF), D_MODEL ** -0.5)
    ffn1_wu = nrm((DEPTH, D_MODEL, D_FF), D_MODEL ** -0.5)
    ffn1_wd = nrm((DEPTH, D_FF, D_MODEL), res_scale * D_FF ** -0.5)
    mix_norm = gain((DEPTH, D_MODEL))
    w_in = nrm((DEPTH, D_MODEL, IN_COLS), D_MODEL ** -0.5)
    b_if = jnp.concatenate([nrm((DEPTH, ML_HEADS), 0.1),
                            jnp.linspace(3.0, 6.0, ML_HEADS, dtype=f32)[None, :] + nrm((DEPTH, ML_HEADS), 0.1)],
                           axis=-1)
    s5_lam_re = -0.5 + nrm((DEPTH, S5_GROUPS, S5_STATE), 0.01)
    s5_lam_im = jnp.pi * n_idx + nrm((DEPTH, S5_GROUPS, S5_STATE), 0.01)
    s5_log_dt = jax.random.uniform(next(ks), (DEPTH, S5_GROUPS), f32,
                                   minval=math.log(DT_MIN), maxval=math.log(DT_MAX))
    s5_b_re = nrm((DEPTH, S5_GROUPS, S5_STATE, S5_GROUP), (2.0 * S5_GROUP) ** -0.5)
    s5_b_im = nrm((DEPTH, S5_GROUPS, S5_STATE, S5_GROUP), (2.0 * S5_GROUP) ** -0.5)
    s5_c_re = nrm((DEPTH, S5_GROUPS, S5_GROUP, S5_STATE), S5_STATE ** -0.5)
    s5_c_im = nrm((DEPTH, S5_GROUPS, S5_GROUP, S5_STATE), S5_STATE ** -0.5)
    s5_d = nrm((DEPTH, S5_WIDTH), 1.0)
    s5_glu_v = nrm((DEPTH, S5_WIDTH, S5_WIDTH), S5_WIDTH ** -0.5)
    s5_glu_g = nrm((DEPTH, S5_WIDTH, S5_WIDTH), S5_WIDTH ** -0.5)
    ml_conv_w = nrm((DEPTH, ML_CONV, ML_WIDTH), ML_CONV ** -0.5)
    ml_conv_b = nrm((DEPTH, ML_WIDTH), 0.02)
    ml_wq = nrm((DEPTH, ML_HEADS, ML_HEAD_DIM, ML_HEAD_DIM), ML_HEAD_DIM ** -0.5)
    ml_wk = nrm((DEPTH, ML_HEADS, ML_HEAD_DIM, ML_HEAD_DIM), ML_HEAD_DIM ** -0.5)
    ml_norm = gain((DEPTH, ML_WIDTH))
    ml_skip = gain((DEPTH, ML_WIDTH))
    w_br_s5 = nrm((DEPTH, S5_WIDTH, D_MODEL), S5_WIDTH ** -0.5)
    w_br_ml = nrm((DEPTH, ML_WIDTH, D_MODEL), ML_WIDTH ** -0.5)
    w_out = nrm((DEPTH, D_MODEL, D_MODEL), res_scale * D_MODEL ** -0.5)
    ffn2_norm = gain((DEPTH, D_MODEL))
    ffn2_wg = nrm((DEPTH, D_MODEL, D_FF), D_MODEL ** -0.5)
    ffn2_wu = nrm((DEPTH, D_MODEL, D_FF), D_MODEL ** -0.5)
    ffn2_wd = nrm((DEPTH, D_FF, D_MODEL), res_scale * D_FF ** -0.5)
    final_norm = gain((D_MODEL,))
    return {'x': x, 'ffn1_norm': ffn1_norm, 'ffn1_wg': ffn1_wg, 'ffn1_wu': ffn1_wu, 'ffn1_wd': ffn1_wd,
            'mix_norm': mix_norm, 'w_in': w_in, 'b_if': b_if,
            's5_lam_re': s5_lam_re, 's5_lam_im': s5_lam_im, 's5_log_dt': s5_log_dt,
            's5_b_re': s5_b_re, 's5_b_im': s5_b_im, 's5_c_re': s5_c_re, 's5_c_im': s5_c_im,
            's5_d': s5_d, 's5_glu_v': s5_glu_v, 's5_glu_g': s5_glu_g,
            'ml_conv_w': ml_conv_w, 'ml_conv_b': ml_conv_b, 'ml_wq': ml_wq, 'ml_wk': ml_wk,
            'ml_norm': ml_norm, 'ml_skip': ml_skip,
            'w_br_s5': w_br_s5, 'w_br_ml': w_br_ml, 'w_out': w_out,
            'ffn2_norm': ffn2_norm, 'ffn2_wg': ffn2_wg, 'ffn2_wu': ffn2_wu, 'ffn2_wd': ffn2_wd,
            'final_norm': final_norm}


def reference(x, ffn1_norm, ffn1_wg, ffn1_wu, ffn1_wd, mix_norm, w_in, b_if,
              s5_lam_re, s5_lam_im, s5_log_dt, s5_b_re, s5_b_im, s5_c_re, s5_c_im,
              s5_d, s5_glu_v, s5_glu_g, ml_conv_w, ml_conv_b, ml_wq, ml_wk, ml_norm, ml_skip,
              w_br_s5, w_br_ml, w_out, ffn2_norm, ffn2_wg, ffn2_wu, ffn2_wd, final_norm):
    for l in range(DEPTH):
        x = x + 0.5 * swiglu(rmsnorm(x, ffn1_norm[l]), ffn1_wg[l], ffn1_wu[l], ffn1_wd[l])
        x = x + hybrid_mixer(rmsnorm(x, mix_norm[l]), w_in[l], b_if[l],
                             s5_lam_re[l], s5_lam_im[l], s5_log_dt[l], s5_b_re[l], s5_b_im[l],
                             s5_c_re[l], s5_c_im[l], s5_d[l], s5_glu_v[l], s5_glu_g[l],
                             ml_conv_w[l], ml_conv_b[l], ml_wq[l], ml_wk[l], ml_norm[l], ml_skip[l],
                             w_br_s5[l], w_br_ml[l], w_out[l])
        x = x + 0.5 * swiglu(rmsnorm(x, ffn2_norm[l]), ffn2_wg[l], ffn2_wu[l], ffn2_wd[l])
    return rmsnorm(x, final_norm)
```

```python
import functools
import math

import jax
import jax.numpy as jnp
from jax import lax
from jax.experimental import pallas as pl
from jax.experimental.pallas import tpu as pltpu

F32 = jnp.float32
BF16 = jnp.bfloat16

NORM_EPS = 1e-6
S5_GROUP = 16
S5_CHUNK = 32
ML_HEADS = 4
ML_CHUNK = 128
GATE_LANES = 128
TOKEN_TILE = 512
FF_CHUNK = 256
VMEM_LIMIT = 56 * 1024 * 1024
HIGHEST = lax.Precision.HIGHEST


def _rms(x, g):
    return x * lax.rsqrt(jnp.mean(x * x, axis=-1, keepdims=True) + NORM_EPS) * g


def _dot(a, b):
    return jnp.dot(a, b, preferred_element_type=F32)


def _const_spec(shape, layer=None):
    nd = len(shape)
    if layer is None:
        return pl.BlockSpec(shape, lambda *_: (0,) * nd, pipeline_mode=pl.Buffered(1))
    return pl.BlockSpec((None,) + tuple(shape), lambda *_: (layer,) + (0,) * nd,
                        pipeline_mode=pl.Buffered(1))


def _params(sem):
    return pltpu.CompilerParams(dimension_semantics=sem, vmem_limit_bytes=VMEM_LIMIT)


def _ffn_kernel(x_ref, g_ref, wg_ref, wu_ref, wd_ref, gf_ref, o_ref, *, final):
    x = x_ref[...]
    h = _rms(x, g_ref[...]).astype(BF16)
    d_ff = wg_ref.shape[1]
    acc = jnp.zeros(x.shape, F32)
    for c in range(d_ff // FF_CHUNK):
        sl = slice(c * FF_CHUNK, (c + 1) * FF_CHUNK)
        g = _dot(h, wg_ref[:, sl])
        u = _dot(h, wu_ref[:, sl])
        a = (g * jax.nn.sigmoid(g) * u).astype(BF16)
        acc = acc + _dot(a, wd_ref[sl, :])
    y = x + 0.5 * acc
    if final:
        y = _rms(y, gf_ref[...])
    o_ref[...] = y


def _ffn(x, norm, wg, wu, wd, final_norm, layer, final):
    n, d = x.shape
    d_ff = wg.shape[2]
    tok = pl.BlockSpec((TOKEN_TILE, d), lambda i: (i, 0))
    return pl.pallas_call(
        functools.partial(_ffn_kernel, final=final),
        grid=(n // TOKEN_TILE,),
        in_specs=[tok, _const_spec((1, d), layer), _const_spec((d, d_ff), layer),
                  _const_spec((d, d_ff), layer), _const_spec((d_ff, d), layer),
                  _const_spec((1, d))],
        out_specs=tok,
        out_shape=jax.ShapeDtypeStruct((n, d), F32),
        compiler_params=_params(("parallel",)),
        name="ffn",
    )(x, norm, wg, wu, wd, final_norm)


def _inproj_kernel(x_ref, g_ref, w_ref, wif_ref, bif_ref, u_ref, xml_ref, v_ref, o_ref, if_ref):
    h = _rms(x_ref[...], g_ref[...]).astype(BF16)
    c0 = u_ref.shape[1]
    c1 = c0 + xml_ref.shape[1]
    c2 = c1 + v_ref.shape[1]
    c3 = c2 + o_ref.shape[1]
    u_ref[...] = _dot(h, w_ref[:, :c0])
    xml_ref[...] = _dot(h, w_ref[:, c0:c1])
    v_ref[...] = _dot(h, w_ref[:, c1:c2])
    o_ref[...] = _dot(h, w_ref[:, c2:c3])
    if_ref[...] = _dot(h, wif_ref[...]) + bif_ref[...]


def _inproj(x, norm, w_main, w_if, b_if, layer, s5_width, ml_width):
    n, d = x.shape
    cols = w_main.shape[2]

    def tok(w):
        return pl.BlockSpec((TOKEN_TILE, w), lambda i: (i, 0))

    widths = (s5_width, ml_width, ml_width, ml_width, GATE_LANES)
    return pl.pallas_call(
        _inproj_kernel,
        grid=(n // TOKEN_TILE,),
        in_specs=[tok(d), _const_spec((1, d), layer), _const_spec((d, cols), layer),
                  _const_spec((d, GATE_LANES), layer), _const_spec((1, GATE_LANES), layer)],
        out_specs=[tok(w) for w in widths],
        out_shape=[jax.ShapeDtypeStruct((n, w), F32) for w in widths],
        compiler_params=_params(("parallel",)),
        name="inproj",
    )(x, norm, w_main, w_if, b_if)


def _s5_tables(lam_re, lam_im, log_dt, b_re, b_im, c_re, c_im):
    T = S5_CHUNK
    G, N, P = b_re.shape
    dt = jnp.exp(log_dt)[:, None]
    lr = jnp.minimum(lam_re, -1e-4)
    li = lam_im
    mag = jnp.exp(lr * dt)
    ab_re = mag * jnp.cos(li * dt)
    ab_im = mag * jnp.sin(li * dt)
    den = lr * lr + li * li
    q_re = ((ab_re - 1.0) * lr + ab_im * li) / den
    q_im = (ab_im * lr - (ab_re - 1.0) * li) / den
    bb_re = q_re[..., None] * b_re - q_im[..., None] * b_im
    bb_im = q_re[..., None] * b_im + q_im[..., None] * b_re
    k = jnp.arange(T + 1, dtype=F32)[:, None, None]
    pmag = jnp.exp(lr * dt * k)
    p_re = pmag * jnp.cos(li * dt * k)
    p_im = pmag * jnp.sin(li * dt * k)
    ca_re = c_re[None] * p_re[:, :, None, :] - c_im[None] * p_im[:, :, None, :]
    ca_im = c_re[None] * p_im[:, :, None, :] + c_im[None] * p_re[:, :, None, :]
    kern = (jnp.einsum('dgpn,gnq->dgpq', ca_re[:T], bb_re, precision=HIGHEST)
            - jnp.einsum('dgpn,gnq->dgpq', ca_im[:T], bb_im, precision=HIGHEST))
    lag = jnp.arange(T)[None, :] - jnp.arange(T)[:, None]
    m = jnp.where((lag >= 0)[:, :, None, None, None], kern[jnp.clip(lag, 0, T - 1)], 0.0)
    m = m.transpose(2, 0, 4, 1, 3).reshape(G, T * P, T * P)
    r_re = p_re[T - 1::-1][:T]
    r_im = p_im[T - 1::-1][:T]
    w_re = r_re[..., None] * bb_re[None] - r_im[..., None] * bb_im[None]
    w_im = r_re[..., None] * bb_im[None] + r_im[..., None] * bb_re[None]
    w = jnp.concatenate([w_re, w_im], axis=2)
    w = w.transpose(1, 0, 3, 2).reshape(G, T * P, 2 * N)
    v = jnp.concatenate([ca_re[1:], -ca_im[1:]], axis=3)
    v = v.transpose(1, 3, 0, 2).reshape(G, 2 * N, T * P)
    a1 = jnp.concatenate([p_re[T], p_re[T]], axis=-1)[:, None, :]
    a2 = jnp.concatenate([-p_im[T], p_im[T]], axis=-1)[:, None, :]
    return m.astype(BF16), w.astype(BF16), v.astype(BF16), a1, a2


def _s5_kernel(u_ref, m_ref, w_ref, v_ref, a1_ref, a2_ref, y_ref, x_scr, s_scr, *, batch):
    u = u_ref[...]
    nc = u.shape[0] // batch
    half = a1_ref.shape[-1] // 2
    x_scr[...] = _dot(u, w_ref[...])
    a1 = a1_ref[...]
    a2 = a2_ref[...]

    def step(c, states):
        new = []
        for b, s in enumerate(states):
            row = b * nc + c
            s_scr[pl.ds(row, 1), :] = s
            new.append(a1 * s + a2 * pltpu.roll(s, half, 1) + x_scr[pl.ds(row, 1), :])
        return tuple(new)

    lax.fori_loop(0, nc, step, tuple(jnp.zeros(a1.shape, F32) for _ in range(batch)))
    y_ref[...] = _dot(u, m_ref[...]) + _dot(s_scr[...].astype(BF16), v_ref[...])


def _s5_scan(uc, m, w, v, a1, a2, batch):
    G, nct, tp = uc.shape
    n2 = w.shape[2]

    def grp(r, c):
        return pl.BlockSpec((None, r, c), lambda g: (g, 0, 0))

    return pl.pallas_call(
        functools.partial(_s5_kernel, batch=batch),
        grid=(G,),
        in_specs=[grp(nct, tp), grp(tp, tp), grp(tp, n2), grp(n2, tp), grp(1, n2), grp(1, n2)],
        out_specs=grp(nct, tp),
        out_shape=jax.ShapeDtypeStruct((G, nct, tp), F32),
        scratch_shapes=[pltpu.VMEM((nct, n2), F32), pltpu.VMEM((nct, n2), F32)],
        compiler_params=_params(("parallel",)),
        name="s5_scan",
    )(uc, m, w, v, a1, a2)


def _shift_rows(x, tail, k):
    xs = pltpu.roll(x, k, 0)
    ts = pltpu.roll(tail, k, 0)
    rows = lax.broadcasted_iota(jnp.int32, ts.shape, 0)
    head = jnp.where(rows < k, ts, xs[:8])
    return jnp.concatenate([head, xs[8:]], axis=0)


def _mlstm_kernel(xml_ref, v_ref, o_ref, if_ref, cw_ref, cb_ref, wq_ref, wk_ref, ng_ref, sk_ref,
                  y_ref, c_scr, n_scr, m_scr, tail_scr):
    H = wq_ref.shape[0]
    dh = wq_ref.shape[1]
    L = xml_ref.shape[0]

    @pl.when(pl.program_id(1) == 0)
    def _():
        c_scr[...] = jnp.zeros(c_scr.shape, F32)
        n_scr[...] = jnp.zeros(n_scr.shape, F32)
        m_scr[...] = jnp.zeros(m_scr.shape, F32)
        tail_scr[...] = jnp.zeros(tail_scr.shape, F32)

    x = xml_ref[...]
    tail = tail_scr[...]
    K = cw_ref.shape[0]
    conv = cb_ref[...]
    for j in range(K):
        d = K - 1 - j
        xd = x if d == 0 else _shift_rows(x, tail, d)
        conv = conv + cw_ref[j:j + 1, :] * xd
    tail_scr[...] = x[L - 8:, :]
    xc = conv * jax.nn.sigmoid(conv)
    xcb = xc.astype(BF16)

    g = if_ref[...]
    lf = jnp.minimum(g, 0.0) - jnp.log1p(jnp.exp(-jnp.abs(g)))
    ri = lax.broadcasted_iota(jnp.int32, (L, L), 0)
    ci = lax.broadcasted_iota(jnp.int32, (L, L), 1)
    causal = ci <= ri
    bcum = jnp.dot(causal.astype(F32), lf, preferred_element_type=F32, precision=HIGHEST)
    bcum_t = bcum.T
    g_t = g.T

    for h in range(H):
        cs = slice(h * dh, (h + 1) * dh)
        q = (_dot(xcb[:, cs], wq_ref[h]) * (dh ** -0.5)).astype(BF16)
        kf = _dot(xcb[:, cs], wk_ref[h])
        vb = v_ref[:, cs].astype(BF16)
        b_col = bcum[:, H + h:H + h + 1]
        i_col = g[:, h:h + 1]
        b_row = bcum_t[H + h:H + h + 1, :]
        i_row = g_t[h:h + 1, :]
        m_prev = m_scr[h, 0:1, 0:1]
        c_prev = c_scr[h]
        n_prev = n_scr[h, 0:1, :]

        dmat = jnp.where(causal, b_col - b_row + i_row, -jnp.inf)
        inter = b_col + m_prev
        m_t = jnp.maximum(inter, jnp.max(dmat, axis=-1, keepdims=True))
        w_inter = jnp.exp(inter - m_t)
        qk = lax.dot_general(q, kf.astype(BF16), (((1,), (1,)), ((), ())), preferred_element_type=F32)
        s = qk * jnp.exp(dmat - m_t)
        num = w_inter * _dot(q, c_prev.astype(BF16)) + _dot(s.astype(BF16), vb)
        qn = jnp.sum(q.astype(F32) * n_prev, axis=-1, keepdims=True)
        den = w_inter * qn + jnp.sum(s, axis=-1, keepdims=True)
        ht = num / jnp.maximum(jnp.abs(den), jnp.exp(-m_t))

        g_tot = b_row[:, L - 1:L]
        dec_row = g_tot - b_row + i_row
        m_new = jnp.maximum(g_tot + m_prev, jnp.max(dec_row, axis=-1, keepdims=True))
        w_old = jnp.exp(g_tot + m_prev - m_new)
        kw = kf * jnp.exp(g_tot - b_col + i_col - m_new)
        c_scr[h] = w_old * c_prev + lax.dot_general(
            kw.astype(BF16), vb, (((0,), (0,)), ((), ())), preferred_element_type=F32)
        n_scr[h] = jnp.broadcast_to(w_old * n_prev + jnp.sum(kw, axis=0, keepdims=True), n_scr.shape[1:])
        m_scr[h] = jnp.broadcast_to(m_new, m_scr.shape[1:])

        hc = jax.nn.sigmoid(o_ref[:, cs]) * ht
        hn = hc * lax.rsqrt(jnp.mean(hc * hc, axis=-1, keepdims=True) + NORM_EPS)
        y_ref[:, cs] = hn * ng_ref[:, cs] + sk_ref[:, cs] * xc[:, cs]


def _mlstm(xml, v, o, ifg, conv_w, conv_b, wq, wk, norm_g, skip, layer):
    B, L, W = xml.shape
    H, dh = wq.shape[1], wq.shape[2]
    K = conv_w.shape[1]

    def seq(w):
        return pl.BlockSpec((None, ML_CHUNK, w), lambda b, c: (b, c, 0))

    return pl.pallas_call(
        _mlstm_kernel,
        grid=(B, L // ML_CHUNK),
        in_specs=[seq(W), seq(W), seq(W), seq(GATE_LANES),
                  _const_spec((K, W), layer), _const_spec((1, W), layer),
                  _const_spec((H, dh, dh), layer), _const_spec((H, dh, dh), layer),
                  _const_spec((1, W), layer), _const_spec((1, W), layer)],
        out_specs=seq(W),
        out_shape=jax.ShapeDtypeStruct((B, L, W), F32),
        scratch_shapes=[pltpu.VMEM((H, dh, dh), F32), pltpu.VMEM((H, 8, dh), F32),
                        pltpu.VMEM((H, 8, 128), F32), pltpu.VMEM((8, W), F32)],
        compiler_params=_params(("parallel", "arbitrary")),
        name="mlstm",
    )(xml, v, o, ifg, conv_w, conv_b, wq, wk, norm_g, skip)


def _gelu_tanh(x):
    return 0.5 * x * (1.0 + jnp.tanh(math.sqrt(2.0 / math.pi) * (x + 0.044715 * (x * x * x))))


def _merge_kernel(x_ref, ys_ref, u_ref, yml_ref, g_ref, wgate_ref, d_ref, gv_ref, gg_ref,
                  wbs_ref, wbm_ref, wout_ref, o_ref):
    x = x_ref[...]
    d = x.shape[1]
    h = _rms(x, g_ref[...]).astype(BF16)
    z = _gelu_tanh(ys_ref[...] + d_ref[...] * u_ref[...]).astype(BF16)
    ys5 = _dot(z, gv_ref[...]) * jax.nn.sigmoid(_dot(z, gg_ref[...]))
    gate_s5 = jax.nn.sigmoid(_dot(h, wgate_ref[:, :d]))
    mix = gate_s5 * _dot(ys5.astype(BF16), wbs_ref[...])
    gate_ml = jax.nn.sigmoid(_dot(h, wgate_ref[:, d:]))
    mix = mix + gate_ml * _dot(yml_ref[...].astype(BF16), wbm_ref[...])
    o_ref[...] = x + _dot(mix.astype(BF16), wout_ref[...])


def _merge(x, ys, u, yml, norm, w_gate, d_skip, gv, gg, wbs, wbm, wout, layer):
    n, d = x.shape
    s5w = ys.shape[1]
    mlw = yml.shape[1]

    def tok(w):
        return pl.BlockSpec((TOKEN_TILE, w), lambda i: (i, 0))

    return pl.pallas_call(
        _merge_kernel,
        grid=(n // TOKEN_TILE,),
        in_specs=[tok(d), tok(s5w), tok(s5w), tok(mlw),
                  _const_spec((1, d), layer), _const_spec((d, 2 * d), layer),
                  _const_spec((1, s5w), layer), _const_spec((s5w, s5w), layer),
                  _const_spec((s5w, s5w), layer), _const_spec((s5w, d), layer),
                  _const_spec((mlw, d), layer), _const_spec((d, d), layer)],
        out_specs=tok(d),
        out_shape=jax.ShapeDtypeStruct((n, d), F32),
        compiler_params=_params(("parallel",)),
        name="merge",
    )(x, ys, u, yml, norm, w_gate, d_skip, gv, gg, wbs, wbm, wout)


def kernel(x, ffn1_norm, ffn1_wg, ffn1_wu, ffn1_wd, mix_norm, w_in, b_if, s5_lam_re, s5_lam_im, s5_log_dt, s5_b_re, s5_b_im, s5_c_re, s5_c_im, s5_d, s5_glu_v, s5_glu_g, ml_conv_w, ml_conv_b, ml_wq, ml_wk, ml_norm, ml_skip, w_br_s5, w_br_ml, w_out, ffn2_norm, ffn2_wg, ffn2_wu, ffn2_wd, final_norm):
    B, L, D = x.shape
    depth = w_in.shape[0]
    s5w = s5_d.shape[1]
    mlw = ml_norm.shape[1]
    H = ml_wq.shape[1]
    G = s5w // S5_GROUP
    T = S5_CHUNK
    o3 = s5w + 3 * mlw
    o4 = o3 + 2 * H

    bf = lambda a: a.astype(BF16)
    row = lambda a: a[:, None, :]
    ffn1 = (row(ffn1_norm), bf(ffn1_wg), bf(ffn1_wu), bf(ffn1_wd))
    ffn2 = (row(ffn2_norm), bf(ffn2_wg), bf(ffn2_wu), bf(ffn2_wd))
    w_main = bf(w_in[:, :, :o3])
    w_if = bf(jnp.pad(w_in[:, :, o3:o4], ((0, 0), (0, 0), (0, GATE_LANES - 2 * H))))
    b_ifp = row(jnp.pad(b_if, ((0, 0), (0, GATE_LANES - 2 * H))))
    w_gate = bf(w_in[:, :, o4:])
    gv, gg, wbs, wbm, wout = bf(s5_glu_v), bf(s5_glu_g), bf(w_br_s5), bf(w_br_ml), bf(w_out)
    wq, wk = bf(ml_wq), bf(ml_wk)
    mixn, d_skip, conv_b, ml_n, ml_s = row(mix_norm), row(s5_d), row(ml_conv_b), row(ml_norm), row(ml_skip)
    fin = final_norm[None, :]
    tables = [_s5_tables(s5_lam_re[l], s5_lam_im[l], s5_log_dt[l], s5_b_re[l], s5_b_im[l],
                         s5_c_re[l], s5_c_im[l]) for l in range(depth)]

    xt = x.reshape(B * L, D)
    for l in range(depth):
        xt = _ffn(xt, *ffn1, fin, l, False)
        u, xml, v, o, ifg = _inproj(xt, mixn, w_main, w_if, b_ifp, l, s5w, mlw)
        uc = bf(u).reshape(B, L // T, T, G, S5_GROUP).transpose(3, 0, 1, 2, 4).reshape(G, B * L // T, T * S5_GROUP)
        yc = _s5_scan(uc, *tables[l], B)
        ys = yc.reshape(G, B, L // T, T, S5_GROUP).transpose(1, 2, 3, 0, 4).reshape(B * L, s5w)
        yml = _mlstm(xml.reshape(B, L, mlw), v.reshape(B, L, mlw), o.reshape(B, L, mlw),
                     ifg.reshape(B, L, GATE_LANES), ml_conv_w, conv_b, wq, wk, ml_n, ml_s, l)
        xt = _merge(xt, ys, u, yml.reshape(B * L, mlw), mixn, w_gate, d_skip, gv, gg, wbs, wbm, wout, l)
        xt = _ffn(xt, *ffn2, fin, l, l == depth - 1)
    return xt.reshape(B, L, D)
```

```python
import functools
import math

import jax
import jax.numpy as jnp
from jax import lax
from jax.experimental import pallas as pl
from jax.experimental.pallas import tpu as pltpu

F32 = jnp.float32
BF16 = jnp.bfloat16

NORM_EPS = 1e-6
S5_GROUP = 16
S5_CHUNK = 32
S5_SEGMENTS = 8
ML_HEADS = 4
ML_CHUNK_S5 = 8
GATE_LANES = 128
TOKEN_TILE = 512
FF_CHUNK = 256
VMEM_LIMIT = 56 * 1024 * 1024
HIGHEST = lax.Precision.HIGHEST


def _rms(x, g):
    return x * lax.rsqrt(jnp.mean(x * x, axis=-1, keepdims=True) + NORM_EPS) * g


def _dot(a, b):
    return jnp.dot(a, b, preferred_element_type=F32)


def _dot_nt(a, b):
    return lax.dot_general(a, b, (((1,), (1,)), ((), ())), preferred_element_type=F32)


def _dot_tn(a, b):
    return lax.dot_general(a, b, (((0,), (0,)), ((), ())), preferred_element_type=F32)


def _const_spec(shape, layer=None):
    nd = len(shape)
    if layer is None:
        return pl.BlockSpec(shape, lambda *_: (0,) * nd, pipeline_mode=pl.Buffered(1))
    return pl.BlockSpec((None,) + tuple(shape), lambda *_: (layer,) + (0,) * nd,
                        pipeline_mode=pl.Buffered(1))


def _params(sem):
    return pltpu.CompilerParams(dimension_semantics=sem, vmem_limit_bytes=VMEM_LIMIT)


def _ffn_kernel(x_ref, g_ref, wg_ref, wu_ref, wd_ref, gf_ref, o_ref, *, final):
    x = x_ref[...]
    h = _rms(x, g_ref[...]).astype(BF16)
    d_ff = wg_ref.shape[1]
    acc = jnp.zeros(x.shape, F32)
    for c in range(d_ff // FF_CHUNK):
        sl = slice(c * FF_CHUNK, (c + 1) * FF_CHUNK)
        g = _dot(h, wg_ref[:, sl])
        u = _dot(h, wu_ref[:, sl])
        a = (g * jax.nn.sigmoid(g) * u).astype(BF16)
        acc = acc + _dot(a, wd_ref[sl, :])
    y = x + 0.5 * acc
    if final:
        y = _rms(y, gf_ref[...])
    o_ref[...] = y


def _ffn(x, norm, wg, wu, wd, final_norm, layer, final):
    n, d = x.shape
    d_ff = wg.shape[2]
    tok = pl.BlockSpec((TOKEN_TILE, d), lambda i: (i, 0))
    return pl.pallas_call(
        functools.partial(_ffn_kernel, final=final),
        grid=(n // TOKEN_TILE,),
        in_specs=[tok, _const_spec((1, d), layer), _const_spec((d, d_ff), layer),
                  _const_spec((d, d_ff), layer), _const_spec((d_ff, d), layer),
                  _const_spec((1, d))],
        out_specs=tok,
        out_shape=jax.ShapeDtypeStruct((n, d), F32),
        compiler_params=_params(("parallel",)),
        name="ffn",
    )(x, norm, wg, wu, wd, final_norm)


def _inproj_kernel(x_ref, g_ref, wut_ref, w_ref, wif_ref, bif_ref, ut_ref, xml_ref, v_ref, o_ref, if_ref):
    h = _rms(x_ref[...], g_ref[...]).astype(BF16)
    c1 = xml_ref.shape[1]
    c2 = c1 + v_ref.shape[1]
    c3 = c2 + o_ref.shape[1]
    ut_ref[...] = _dot_nt(wut_ref[...], h)
    xml_ref[...] = _dot(h, w_ref[:, :c1])
    v_ref[...] = _dot(h, w_ref[:, c1:c2])
    o_ref[...] = _dot(h, w_ref[:, c2:c3])
    if_ref[...] = _dot(h, wif_ref[...]) + bif_ref[...]


def _inproj(x, norm, w_ut, w_main, w_if, b_if, layer, nc):
    n, d = x.shape
    s5w = w_ut.shape[1]
    cols = w_main.shape[2]
    mlw = cols // 3

    def tok(w):
        return pl.BlockSpec((nc, w), lambda i: (i, 0))

    widths = (mlw, mlw, mlw, GATE_LANES)
    return pl.pallas_call(
        _inproj_kernel,
        grid=(n // nc,),
        in_specs=[tok(d), _const_spec((1, d), layer), _const_spec((s5w, d), layer),
                  _const_spec((d, cols), layer), _const_spec((d, GATE_LANES), layer),
                  _const_spec((1, GATE_LANES), layer)],
        out_specs=[pl.BlockSpec((None, s5w, nc), lambda i: (i, 0, 0))] + [tok(w) for w in widths],
        out_shape=[jax.ShapeDtypeStruct((n // nc, s5w, nc), F32)]
        + [jax.ShapeDtypeStruct((n, w), F32) for w in widths],
        compiler_params=_params(("parallel",)),
        name="inproj",
    )(x, norm, w_ut, w_main, w_if, b_if)


def _s5_tables(lam_re, lam_im, log_dt, b_re, b_im, c_re, c_im, seg_len):
    T = S5_CHUNK
    G, N, P = b_re.shape
    dt = jnp.exp(log_dt)[:, None]
    lr = jnp.minimum(lam_re, -1e-4)
    li = lam_im
    mag = jnp.exp(lr * dt)
    ab_re = mag * jnp.cos(li * dt)
    ab_im = mag * jnp.sin(li * dt)
    den = lr * lr + li * li
    q_re = ((ab_re - 1.0) * lr + ab_im * li) / den
    q_im = (ab_im * lr - (ab_re - 1.0) * li) / den
    bb_re = q_re[..., None] * b_re - q_im[..., None] * b_im
    bb_im = q_re[..., None] * b_im + q_im[..., None] * b_re

    def power(k):
        pm = jnp.exp(lr * dt * k)
        return pm * jnp.cos(li * dt * k), pm * jnp.sin(li * dt * k)

    p_re, p_im = power(jnp.arange(T + 1, dtype=F32)[:, None, None])
    ca_re = c_re[None] * p_re[:, :, None, :] - c_im[None] * p_im[:, :, None, :]
    ca_im = c_re[None] * p_im[:, :, None, :] + c_im[None] * p_re[:, :, None, :]
    kern = (jnp.einsum('dgpn,gnq->dgpq', ca_re[:T], bb_re, precision=HIGHEST)
            - jnp.einsum('dgpn,gnq->dgpq', ca_im[:T], bb_im, precision=HIGHEST))
    lag = jnp.arange(T)[:, None] - jnp.arange(T)[None, :]
    m = jnp.where((lag >= 0)[:, :, None, None, None], kern[jnp.clip(lag, 0, T - 1)], 0.0)
    m = m.transpose(2, 0, 3, 1, 4).reshape(G, T * P, T * P)
    r_re = p_re[T - 1::-1]
    r_im = p_im[T - 1::-1]
    w_re = r_re[..., None] * bb_re[None] - r_im[..., None] * bb_im[None]
    w_im = r_re[..., None] * bb_im[None] + r_im[..., None] * bb_re[None]
    w = jnp.concatenate([w_re, w_im, w_im, w_re], axis=2)
    w = w.transpose(1, 2, 0, 3).reshape(G, 4 * N, T * P)
    v = jnp.concatenate([ca_re[1:], -ca_im[1:]], axis=3)
    v = v.transpose(1, 0, 2, 3).reshape(G, T * P, 2 * N)

    def forms(re, im):
        return (jnp.concatenate([re, re], -1), jnp.concatenate([-im, im], -1), jnp.concatenate([im, -im], -1))

    seg = power(float(T * seg_len))
    coef = jnp.stack(forms(p_re[T], p_im[T]) + forms(*seg) + (jnp.zeros((G, 2 * N), F32),) * 2, axis=1)
    c_pw = power(T * jnp.arange(seg_len, dtype=F32)[:, None, None])
    f1, f2, _ = forms(*c_pw)
    pw = jnp.concatenate([f1, f2], axis=0).transpose(1, 0, 2)
    return m.astype(BF16), w.astype(BF16), v.astype(BF16), coef, pw


def _s5_kernel(u_ref, m_ref, w_ref, v_ref, coef_ref, pw_ref, y_ref, xs_scr, xt_scr, sp_scr, *, batch):
    T, P, nct = u_ref.shape
    n2 = sp_scr.shape[1]
    ncb = nct // batch
    R = ncb // S5_SEGMENTS
    u = u_ref[...].reshape(T * P, nct).astype(BF16)
    x = _dot(w_ref[...], u).T
    xs_scr[...] = x[:, :n2]
    xt_scr[...] = x[:, n2:]
    c1, c2, c2t = coef_ref[0:1, :], coef_ref[1:2, :], coef_ref[2:3, :]
    g1, g2, g2t = coef_ref[3:4, :], coef_ref[4:5, :], coef_ref[5:6, :]
    rows = lax.broadcasted_iota(jnp.int32, (S5_SEGMENTS, n2), 0)

    def seg_rows(b, r):
        return pl.ds(b * ncb + r, S5_SEGMENTS, stride=R)

    for b in range(batch):
        s = jnp.zeros((S5_SEGMENTS, n2), F32)
        t = jnp.zeros((S5_SEGMENTS, n2), F32)
        for r in range(R):
            sp_scr[seg_rows(b, r), :] = s
            s, t = (c1 * s + c2 * t + xs_scr[seg_rows(b, r), :],
                    c1 * t + c2t * s + xt_scr[seg_rows(b, r), :])
        ini_s = jnp.zeros((S5_SEGMENTS, n2), F32)
        ini_t = jnp.zeros((S5_SEGMENTS, n2), F32)
        cur_s = jnp.zeros((1, n2), F32)
        cur_t = jnp.zeros((1, n2), F32)
        for k in range(1, S5_SEGMENTS):
            cur_s, cur_t = (g1 * cur_s + g2 * cur_t + s[k - 1:k, :],
                            g1 * cur_t + g2t * cur_s + t[k - 1:k, :])
            ini_s = jnp.where(rows == k, cur_s, ini_s)
            ini_t = jnp.where(rows == k, cur_t, ini_t)
        for r in range(R):
            sp_scr[seg_rows(b, r), :] = (sp_scr[seg_rows(b, r), :] + pw_ref[r:r + 1, :] * ini_s
                                         + pw_ref[R + r:R + r + 1, :] * ini_t)
    y = _dot(m_ref[...], u) + _dot_nt(v_ref[...], sp_scr[...].astype(BF16))
    y_ref[...] = y.reshape(T, P, nct)


def _s5_scan(ut, m, w, v, coef, pw, batch):
    T, s5w, nct = ut.shape
    G = s5w // S5_GROUP
    tp = T * S5_GROUP
    n2 = v.shape[2]

    def grp(r, c):
        return pl.BlockSpec((None, r, c), lambda g: (g, 0, 0))

    slab = pl.BlockSpec((T, S5_GROUP, nct), lambda g: (0, g, 0))
    return pl.pallas_call(
        functools.partial(_s5_kernel, batch=batch),
        grid=(G,),
        in_specs=[slab, grp(tp, tp), grp(2 * n2, tp), grp(tp, n2), grp(8, n2), grp(pw.shape[1], n2)],
        out_specs=slab,
        out_shape=jax.ShapeDtypeStruct((T, s5w, nct), F32),
        scratch_shapes=[pltpu.VMEM((nct, n2), F32)] * 3,
        compiler_params=_params(("parallel",)),
        name="s5_scan",
    )(ut, m, w, v, coef, pw)


def _mlstm_kernel(xml_ref, v_ref, o_ref, if_ref, cw_ref, cb_ref, wq_ref, wk_ref, ng_ref, sk_ref,
                  y_ref, c_scr, n_scr, m_scr, tail_scr):
    H = wq_ref.shape[0]
    dh = wq_ref.shape[1]
    T, CS, W = xml_ref.shape
    L = T * CS
    K = cw_ref.shape[0]

    @pl.when(pl.program_id(1) == 0)
    def _():
        c_scr[...] = jnp.zeros(c_scr.shape, F32)
        n_scr[...] = jnp.zeros(n_scr.shape, F32)
        m_scr[...] = jnp.zeros(m_scr.shape, F32)
        tail_scr[...] = jnp.zeros(tail_scr.shape, F32)

    x = xml_ref[...].reshape(L, W)
    nw = (K - 1) * CS
    last = x[L - nw:, :]
    prev = tail_scr[...]
    crow = lax.broadcasted_iota(jnp.int32, (nw, W), 0) % CS
    wrap = jnp.where(crow == 0, pltpu.roll(prev, nw - (CS - 1), 0), pltpu.roll(last, 1, 0))
    tail_scr[...] = last
    conv = cb_ref[...] + cw_ref[K - 1:K, :] * x
    for d in range(1, K):
        xd = jnp.concatenate([wrap[nw - d * CS:, :], x[:L - d * CS, :]], axis=0)
        conv = conv + cw_ref[K - 1 - d:K - d, :] * xd
    xc = conv * jax.nn.sigmoid(conv)
    xcb = xc.astype(BF16)

    g = if_ref[...].reshape(L, if_ref.shape[2])
    lf = jnp.minimum(g, 0.0) - jnp.log1p(jnp.exp(-jnp.abs(g)))
    ri = lax.broadcasted_iota(jnp.int32, (L, L), 0)
    ci = lax.broadcasted_iota(jnp.int32, (L, L), 1)
    causal = ((ci % CS) * T + ci // CS) <= ((ri % CS) * T + ri // CS)
    bcum = jnp.dot(causal.astype(F32), lf, preferred_element_type=F32, precision=HIGHEST)
    bcum_t = bcum.T
    g_t = g.T

    for h in range(H):
        cs = slice(h * dh, (h + 1) * dh)
        q = (_dot(xcb[:, cs], wq_ref[h]) * (dh ** -0.5)).astype(BF16)
        kf = _dot(xcb[:, cs], wk_ref[h])
        vb = v_ref[:, :, cs].reshape(L, dh).astype(BF16)
        b_col = bcum[:, H + h:H + h + 1]
        i_col = g[:, h:h + 1]
        b_row = bcum_t[H + h:H + h + 1, :]
        i_row = g_t[h:h + 1, :]
        m_prev = m_scr[h, 0:1, 0:1]
        c_prev = c_scr[h]
        n_prev = n_scr[h, 0:1, :]

        dmat = jnp.where(causal, b_col - b_row + i_row, -jnp.inf)
        inter = b_col + m_prev
        m_t = jnp.maximum(inter, jnp.max(dmat, axis=-1, keepdims=True))
        w_inter = jnp.exp(inter - m_t)
        s = _dot_nt(q, kf.astype(BF16)) * jnp.exp(dmat - m_t)
        num = w_inter * _dot(q, c_prev.astype(BF16)) + _dot(s.astype(BF16), vb)
        qn = jnp.sum(q.astype(F32) * n_prev, axis=-1, keepdims=True)
        den = w_inter * qn + jnp.sum(s, axis=-1, keepdims=True)
        ht = num / jnp.maximum(jnp.abs(den), jnp.exp(-m_t))

        g_tot = b_row[:, L - 1:L]
        dec_row = g_tot - b_row + i_row
        m_new = jnp.maximum(g_tot + m_prev, jnp.max(dec_row, axis=-1, keepdims=True))
        w_old = jnp.exp(g_tot + m_prev - m_new)
        kw = kf * jnp.exp(g_tot - b_col + i_col - m_new)
        c_scr[h] = w_old * c_prev + _dot_tn(kw.astype(BF16), vb)
        n_scr[h] = jnp.broadcast_to(w_old * n_prev + jnp.sum(kw, axis=0, keepdims=True), n_scr.shape[1:])
        m_scr[h] = jnp.broadcast_to(m_new, m_scr.shape[1:])

        hc = jax.nn.sigmoid(o_ref[:, :, cs].reshape(L, dh)) * ht
        hn = hc * lax.rsqrt(jnp.mean(hc * hc, axis=-1, keepdims=True) + NORM_EPS)
        y_ref[:, :, cs] = (hn * ng_ref[:, cs] + sk_ref[:, cs] * xc[:, cs]).reshape(T, CS, dh)


def _mlstm(xml, v, o, ifg, conv_w, conv_b, wq, wk, norm_g, skip, layer, batch):
    T, B, ncb, W = xml.shape
    H, dh = wq.shape[1], wq.shape[2]
    K = conv_w.shape[1]
    CS = ML_CHUNK_S5

    def seq(w):
        return pl.BlockSpec((T, None, CS, w), lambda b, c: (0, b, c, 0))

    return pl.pallas_call(
        _mlstm_kernel,
        grid=(B, ncb // CS),
        in_specs=[seq(W), seq(W), seq(W), seq(GATE_LANES),
                  _const_spec((K, W), layer), _const_spec((1, W), layer),
                  _const_spec((H, dh, dh), layer), _const_spec((H, dh, dh), layer),
                  _const_spec((1, W), layer), _const_spec((1, W), layer)],
        out_specs=seq(W),
        out_shape=jax.ShapeDtypeStruct((T, B, ncb, W), F32),
        scratch_shapes=[pltpu.VMEM((H, dh, dh), F32), pltpu.VMEM((H, 8, dh), F32),
                        pltpu.VMEM((H, 8, 128), F32), pltpu.VMEM(((K - 1) * CS, W), F32)],
        compiler_params=_params(("parallel", "arbitrary")),
        name="mlstm",
    )(xml, v, o, ifg, conv_w, conv_b, wq, wk, norm_g, skip)


def _gelu_tanh(x):
    return 0.5 * x * (1.0 + jnp.tanh(math.sqrt(2.0 / math.pi) * (x + 0.044715 * (x * x * x))))


def _merge_kernel(x_ref, yt_ref, ut_ref, yml_ref, g_ref, wgate_ref, d_ref, gv_ref, gg_ref,
                  wbs_ref, wbm_ref, wout_ref, o_ref):
    x = x_ref[...]
    d = x.shape[1]
    h = _rms(x, g_ref[...]).astype(BF16)
    z = _gelu_tanh(yt_ref[...] + d_ref[...] * ut_ref[...]).T.astype(BF16)
    ys5 = _dot(z, gv_ref[...]) * jax.nn.sigmoid(_dot(z, gg_ref[...]))
    gate_s5 = jax.nn.sigmoid(_dot(h, wgate_ref[:, :d]))
    mix = gate_s5 * _dot(ys5.astype(BF16), wbs_ref[...])
    gate_ml = jax.nn.sigmoid(_dot(h, wgate_ref[:, d:]))
    mix = mix + gate_ml * _dot(yml_ref[...].astype(BF16), wbm_ref[...])
    o_ref[...] = x + _dot(mix.astype(BF16), wout_ref[...])


def _merge(x, yt, ut, yml, norm, w_gate, d_skip, gv, gg, wbs, wbm, wout, layer):
    n, d = x.shape
    T, s5w, nc = yt.shape
    mlw = yml.shape[1]

    def tok(w):
        return pl.BlockSpec((nc, w), lambda i: (i, 0))

    slab = pl.BlockSpec((None, s5w, nc), lambda i: (i, 0, 0))
    return pl.pallas_call(
        _merge_kernel,
        grid=(T,),
        in_specs=[tok(d), slab, slab, tok(mlw),
                  _const_spec((1, d), layer), _const_spec((d, 2 * d), layer),
                  _const_spec((s5w, 1), layer), _const_spec((s5w, s5w), layer),
                  _const_spec((s5w, s5w), layer), _const_spec((s5w, d), layer),
                  _const_spec((mlw, d), layer), _const_spec((d, d), layer)],
        out_specs=tok(d),
        out_shape=jax.ShapeDtypeStruct((n, d), F32),
        compiler_params=_params(("parallel",)),
        name="merge",
    )(x, yt, ut, yml, norm, w_gate, d_skip, gv, gg, wbs, wbm, wout)


def kernel(x, ffn1_norm, ffn1_wg, ffn1_wu, ffn1_wd, mix_norm, w_in, b_if, s5_lam_re, s5_lam_im, s5_log_dt, s5_b_re, s5_b_im, s5_c_re, s5_c_im, s5_d, s5_glu_v, s5_glu_g, ml_conv_w, ml_conv_b, ml_wq, ml_wk, ml_norm, ml_skip, w_br_s5, w_br_ml, w_out, ffn2_norm, ffn2_wg, ffn2_wu, ffn2_wd, final_norm):
    B, L, D = x.shape
    depth = w_in.shape[0]
    s5w = s5_d.shape[1]
    mlw = ml_norm.shape[1]
    H = ml_wq.shape[1]
    T = S5_CHUNK
    ncb = L // T
    nc = B * ncb
    assert ncb % S5_SEGMENTS == 0 and ncb % ML_CHUNK_S5 == 0 and (T * nc) % TOKEN_TILE == 0
    o3 = s5w + 3 * mlw
    o4 = o3 + 2 * H

    bf = lambda a: a.astype(BF16)
    row = lambda a: a[:, None, :]
    ffn1 = (row(ffn1_norm), bf(ffn1_wg), bf(ffn1_wu), bf(ffn1_wd))
    ffn2 = (row(ffn2_norm), bf(ffn2_wg), bf(ffn2_wu), bf(ffn2_wd))
    w_ut = bf(jnp.swapaxes(w_in[:, :, :s5w], 1, 2))
    w_main = bf(w_in[:, :, s5w:o3])
    w_if = bf(jnp.pad(w_in[:, :, o3:o4], ((0, 0), (0, 0), (0, GATE_LANES - 2 * H))))
    b_ifp = row(jnp.pad(b_if, ((0, 0), (0, GATE_LANES - 2 * H))))
    w_gate = bf(w_in[:, :, o4:])
    gv, gg, wbs, wbm, wout = bf(s5_glu_v), bf(s5_glu_g), bf(w_br_s5), bf(w_br_ml), bf(w_out)
    wq, wk = bf(ml_wq), bf(ml_wk)
    mixn, conv_b, ml_n, ml_s = row(mix_norm), row(ml_conv_b), row(ml_norm), row(ml_skip)
    d_skip = s5_d[:, :, None]
    fin = final_norm[None, :]
    tables = [_s5_tables(s5_lam_re[l], s5_lam_im[l], s5_log_dt[l], s5_b_re[l], s5_b_im[l],
                         s5_c_re[l], s5_c_im[l], ncb // S5_SEGMENTS) for l in range(depth)]

    xt = x.reshape(B, ncb, T, D).transpose(2, 0, 1, 3).reshape(T * nc, D)
    seq4 = lambda a: a.reshape(T, B, ncb, a.shape[-1])
    for l in range(depth):
        xt = _ffn(xt, *ffn1, fin, l, False)
        ut, xml, v, o, ifg = _inproj(xt, mixn, w_ut, w_main, w_if, b_ifp, l, nc)
        yt = _s5_scan(ut, *tables[l], B)
        yml = _mlstm(seq4(xml), seq4(v), seq4(o), seq4(ifg), ml_conv_w, conv_b, wq, wk, ml_n, ml_s, l, B)
        xt = _merge(xt, yt, ut, yml.reshape(T * nc, mlw), mixn, w_gate, d_skip, gv, gg, wbs, wbm, wout, l)
        xt = _ffn(xt, *ffn2, fin, l, l == depth - 1)
    return xt.reshape(T, B, ncb, D).transpose(1, 2, 0, 3).reshape(B, L, D)
```

```python
import functools
import math

import jax
import jax.numpy as jnp
from jax import lax
from jax.experimental import pallas as pl
from jax.experimental.pallas import tpu as pltpu

F32 = jnp.float32
BF16 = jnp.bfloat16

NORM_EPS = 1e-6
S5_GROUP = 16
S5_CHUNK = 32
S5_SEGMENTS = 8
ML_HEADS = 4
ML_CHUNK_S5 = 8
GATE_LANES = 128
TOKEN_TILE = 512
FF_CHUNK = 256
VMEM_LIMIT = 56 * 1024 * 1024
HIGHEST = lax.Precision.HIGHEST


def _rms(x, g):
    return x * lax.rsqrt(jnp.mean(x * x, axis=-1, keepdims=True) + NORM_EPS) * g


def _dot(a, b):
    return jnp.dot(a, b, preferred_element_type=F32)


def _dot_nt(a, b):
    return lax.dot_general(a, b, (((1,), (1,)), ((), ())), preferred_element_type=F32)


def _dot_tn(a, b):
    return lax.dot_general(a, b, (((0,), (0,)), ((), ())), preferred_element_type=F32)


def _const_spec(shape, layer=None):
    nd = len(shape)
    if layer is None:
        return pl.BlockSpec(shape, lambda *_: (0,) * nd, pipeline_mode=pl.Buffered(1))
    return pl.BlockSpec((None,) + tuple(shape), lambda *_: (layer,) + (0,) * nd,
                        pipeline_mode=pl.Buffered(1))


def _params(sem):
    return pltpu.CompilerParams(dimension_semantics=sem, vmem_limit_bytes=VMEM_LIMIT)


def _ffn_kernel(x_ref, g_ref, wg_ref, wu_ref, wd_ref, gf_ref, o_ref, *, final):
    x = x_ref[...]
    h = _rms(x, g_ref[...]).astype(BF16)
    d_ff = wg_ref.shape[1]
    acc = jnp.zeros(x.shape, F32)
    for c in range(d_ff // FF_CHUNK):
        sl = slice(c * FF_CHUNK, (c + 1) * FF_CHUNK)
        g = _dot(h, wg_ref[:, sl].astype(BF16))
        u = _dot(h, wu_ref[:, sl].astype(BF16))
        a = (g * jax.nn.sigmoid(g) * u).astype(BF16)
        acc = acc + _dot(a, wd_ref[sl, :].astype(BF16))
    y = x + 0.5 * acc
    if final:
        y = _rms(y, gf_ref[...])
    o_ref[...] = y


def _ffn(x, norm, wg, wu, wd, final_norm, layer, final):
    n, d = x.shape
    d_ff = wg.shape[2]
    tok = pl.BlockSpec((TOKEN_TILE, d), lambda i: (i, 0))
    return pl.pallas_call(
        functools.partial(_ffn_kernel, final=final),
        grid=(n // TOKEN_TILE,),
        in_specs=[tok, _const_spec((1, d), layer), _const_spec((d, d_ff), layer),
                  _const_spec((d, d_ff), layer), _const_spec((d_ff, d), layer),
                  _const_spec((1, d))],
        out_specs=tok,
        out_shape=jax.ShapeDtypeStruct((n, d), F32),
        compiler_params=_params(("parallel",)),
        name="ffn",
    )(x, norm, wg, wu, wd, final_norm)


def _inproj_kernel(x_ref, g_ref, w_ref, bif_ref, ut_ref, xml_ref, v_ref, o_ref, if_ref, *, n_gates):
    h = _rms(x_ref[...], g_ref[...]).astype(BF16)
    c0 = ut_ref.shape[0]
    c1 = c0 + xml_ref.shape[1]
    c2 = c1 + v_ref.shape[1]
    c3 = c2 + o_ref.shape[1]
    gl = if_ref.shape[1]
    ut_ref[...] = _dot(h, w_ref[:, :c0].astype(BF16)).T
    xml_ref[...] = _dot(h, w_ref[:, c0:c1].astype(BF16))
    v_ref[...] = _dot(h, w_ref[:, c1:c2].astype(BF16))
    o_ref[...] = _dot(h, w_ref[:, c2:c3].astype(BF16))
    gates = _dot(h, w_ref[:, c3:c3 + gl].astype(BF16))
    lane = lax.broadcasted_iota(jnp.int32, gates.shape, 1)
    if_ref[...] = jnp.where(lane < n_gates, gates, 0.0) + bif_ref[...]


def _inproj(x, norm, w_in, b_if, layer, nc, s5w, mlw, n_gates):
    n, d = x.shape
    cols = w_in.shape[2]
    assert (s5w + 3 * mlw) % GATE_LANES == 0 and s5w + 3 * mlw + GATE_LANES <= cols

    def tok(w):
        return pl.BlockSpec((nc, w), lambda i: (i, 0))

    widths = (mlw, mlw, mlw, GATE_LANES)
    return pl.pallas_call(
        functools.partial(_inproj_kernel, n_gates=n_gates),
        grid=(n // nc,),
        in_specs=[tok(d), _const_spec((1, d), layer), _const_spec((d, cols), layer),
                  _const_spec((1, GATE_LANES), layer)],
        out_specs=[pl.BlockSpec((None, s5w, nc), lambda i: (i, 0, 0))] + [tok(w) for w in widths],
        out_shape=[jax.ShapeDtypeStruct((n // nc, s5w, nc), F32)]
        + [jax.ShapeDtypeStruct((n, w), F32) for w in widths],
        compiler_params=_params(("parallel",)),
        name="inproj",
    )(x, norm, w_in, b_if)


def _s5_tables(lam_re, lam_im, log_dt, b_re, b_im, c_re, c_im, seg_len):
    T = S5_CHUNK
    G, N, P = b_re.shape
    dt = jnp.exp(log_dt)[:, None]
    lr = jnp.minimum(lam_re, -1e-4)
    li = lam_im
    mag = jnp.exp(lr * dt)
    ab_re = mag * jnp.cos(li * dt)
    ab_im = mag * jnp.sin(li * dt)
    den = lr * lr + li * li
    q_re = ((ab_re - 1.0) * lr + ab_im * li) / den
    q_im = (ab_im * lr - (ab_re - 1.0) * li) / den
    bb_re = q_re[..., None] * b_re - q_im[..., None] * b_im
    bb_im = q_re[..., None] * b_im + q_im[..., None] * b_re

    def power(k):
        pm = jnp.exp(lr * dt * k)
        return pm * jnp.cos(li * dt * k), pm * jnp.sin(li * dt * k)

    p_re, p_im = power(jnp.arange(T + 1, dtype=F32)[:, None, None])
    ca_re = c_re[None] * p_re[:, :, None, :] - c_im[None] * p_im[:, :, None, :]
    ca_im = c_re[None] * p_im[:, :, None, :] + c_im[None] * p_re[:, :, None, :]
    kern = (jnp.einsum('dgpn,gnq->dgpq', ca_re[:T], bb_re, precision=HIGHEST)
            - jnp.einsum('dgpn,gnq->dgpq', ca_im[:T], bb_im, precision=HIGHEST))
    kcol = kern.transpose(1, 0, 2, 3).reshape(G, T * P, P)
    r_re = p_re[T - 1::-1]
    r_im = p_im[T - 1::-1]
    w_re = r_re[..., None] * bb_re[None] - r_im[..., None] * bb_im[None]
    w_im = r_re[..., None] * bb_im[None] + r_im[..., None] * bb_re[None]
    w = jnp.concatenate([w_re, w_im, w_im, w_re], axis=2)
    w = w.transpose(1, 2, 0, 3).reshape(G, 4 * N, T * P)
    v = jnp.concatenate([ca_re[1:], -ca_im[1:]], axis=3)
    v = v.transpose(1, 0, 2, 3).reshape(G, T * P, 2 * N)

    def forms(re, im):
        return (jnp.concatenate([re, re], -1), jnp.concatenate([-im, im], -1), jnp.concatenate([im, -im], -1))

    seg = power(float(T * seg_len))
    coef = jnp.stack(forms(p_re[T], p_im[T]) + forms(*seg) + (jnp.zeros((G, 2 * N), F32),) * 2, axis=1)
    c_pw = power(T * jnp.arange(seg_len, dtype=F32)[:, None, None])
    f1, f2, _ = forms(*c_pw)
    pw = jnp.concatenate([f1, f2], axis=0).transpose(1, 0, 2)
    return kcol.astype(BF16), w.astype(BF16), v.astype(BF16), coef, pw


def _toeplitz(kcol, T, P):
    tp = T * P
    lane = lax.broadcasted_iota(jnp.int32, (P, tp), 1)
    sub = lax.broadcasted_iota(jnp.int32, (P, tp), 0)
    rep = jnp.where(lane % P == sub, 1.0, 0.0).astype(BF16)
    m = _dot(kcol, rep)
    blk = lax.broadcasted_iota(jnp.int32, (tp, tp), 1) // P
    shift = P
    while shift < tp:
        moved = jnp.concatenate([jnp.zeros((shift, tp), F32), m[:tp - shift, :]], axis=0)
        m = jnp.where((blk & (shift // P)) != 0, moved, m)
        shift *= 2
    return m.astype(BF16)


def _s5_kernel(u_ref, kcol_ref, w_ref, v_ref, coef_ref, pw_ref, y_ref, xs_scr, xt_scr, sp_scr, *, batch):
    T, P, nct = u_ref.shape
    n2 = sp_scr.shape[1]
    ncb = nct // batch
    R = ncb // S5_SEGMENTS
    u = u_ref[...].reshape(T * P, nct).astype(BF16)
    m = _toeplitz(kcol_ref[...], T, P)
    x = _dot(w_ref[...], u).T
    xs_scr[...] = x[:, :n2]
    xt_scr[...] = x[:, n2:]
    c1, c2, c2t = coef_ref[0:1, :], coef_ref[1:2, :], coef_ref[2:3, :]
    g1, g2, g2t = coef_ref[3:4, :], coef_ref[4:5, :], coef_ref[5:6, :]
    rows = lax.broadcasted_iota(jnp.int32, (S5_SEGMENTS, n2), 0)

    def seg_rows(b, r):
        return pl.ds(b * ncb + r, S5_SEGMENTS, stride=R)

    for b in range(batch):
        s = jnp.zeros((S5_SEGMENTS, n2), F32)
        t = jnp.zeros((S5_SEGMENTS, n2), F32)
        for r in range(R):
            sp_scr[seg_rows(b, r), :] = s
            s, t = (c1 * s + c2 * t + xs_scr[seg_rows(b, r), :],
                    c1 * t + c2t * s + xt_scr[seg_rows(b, r), :])
        ini_s = jnp.zeros((S5_SEGMENTS, n2), F32)
        ini_t = jnp.zeros((S5_SEGMENTS, n2), F32)
        cur_s = jnp.zeros((1, n2), F32)
        cur_t = jnp.zeros((1, n2), F32)
        for k in range(1, S5_SEGMENTS):
            cur_s, cur_t = (g1 * cur_s + g2 * cur_t + s[k - 1:k, :],
                            g1 * cur_t + g2t * cur_s + t[k - 1:k, :])
            ini_s = jnp.where(rows == k, cur_s, ini_s)
            ini_t = jnp.where(rows == k, cur_t, ini_t)
        for r in range(R):
            sp_scr[seg_rows(b, r), :] = (sp_scr[seg_rows(b, r), :] + pw_ref[r:r + 1, :] * ini_s
                                         + pw_ref[R + r:R + r + 1, :] * ini_t)
    y = _dot(m, u) + _dot_nt(v_ref[...], sp_scr[...].astype(BF16))
    y_ref[...] = y.reshape(T, P, nct)


def _s5_scan(ut, kcol, w, v, coef, pw, batch):
    T, s5w, nct = ut.shape
    G = s5w // S5_GROUP
    tp = T * S5_GROUP
    n2 = v.shape[2]

    def grp(r, c):
        return pl.BlockSpec((None, r, c), lambda g: (g, 0, 0))

    slab = pl.BlockSpec((T, S5_GROUP, nct), lambda g: (0, g, 0))
    return pl.pallas_call(
        functools.partial(_s5_kernel, batch=batch),
        grid=(G,),
        in_specs=[slab, grp(tp, S5_GROUP), grp(2 * n2, tp), grp(tp, n2), grp(8, n2), grp(pw.shape[1], n2)],
        out_specs=slab,
        out_shape=jax.ShapeDtypeStruct((T, s5w, nct), F32),
        scratch_shapes=[pltpu.VMEM((nct, n2), F32)] * 3,
        compiler_params=_params(("parallel",)),
        name="s5_scan",
    )(ut, kcol, w, v, coef, pw)


def _mlstm_kernel(xml_ref, v_ref, o_ref, if_ref, cw_ref, cb_ref, wq_ref, wk_ref, ng_ref, sk_ref,
                  y_ref, c_scr, n_scr, m_scr, tail_scr):
    H = wq_ref.shape[0]
    dh = wq_ref.shape[1]
    T, CS, W = xml_ref.shape
    L = T * CS
    K = cw_ref.shape[0]

    @pl.when(pl.program_id(1) == 0)
    def _():
        c_scr[...] = jnp.zeros(c_scr.shape, F32)
        n_scr[...] = jnp.zeros(n_scr.shape, F32)
        m_scr[...] = jnp.zeros(m_scr.shape, F32)
        tail_scr[...] = jnp.zeros(tail_scr.shape, F32)

    x = xml_ref[...].reshape(L, W)
    nw = (K - 1) * CS
    last = x[L - nw:, :]
    prev = tail_scr[...]
    crow = lax.broadcasted_iota(jnp.int32, (nw, W), 0) % CS
    wrap = jnp.where(crow == 0, pltpu.roll(prev, nw - (CS - 1), 0), pltpu.roll(last, 1, 0))
    tail_scr[...] = last
    conv = cb_ref[...] + cw_ref[K - 1:K, :] * x
    for d in range(1, K):
        xd = jnp.concatenate([wrap[nw - d * CS:, :], x[:L - d * CS, :]], axis=0)
        conv = conv + cw_ref[K - 1 - d:K - d, :] * xd
    xc = conv * jax.nn.sigmoid(conv)
    xcb = xc.astype(BF16)

    g = if_ref[...].reshape(L, if_ref.shape[2])
    lf = jnp.minimum(g, 0.0) - jnp.log1p(jnp.exp(-jnp.abs(g)))
    ri = lax.broadcasted_iota(jnp.int32, (L, L), 0)
    ci = lax.broadcasted_iota(jnp.int32, (L, L), 1)
    causal = ((ci % CS) * T + ci // CS) <= ((ri % CS) * T + ri // CS)
    bcum = jnp.dot(causal.astype(F32), lf, preferred_element_type=F32, precision=HIGHEST)
    bcum_t = bcum.T
    g_t = g.T

    for h in range(H):
        cs = slice(h * dh, (h + 1) * dh)
        q = (_dot(xcb[:, cs], wq_ref[h].astype(BF16)) * (dh ** -0.5)).astype(BF16)
        kf = _dot(xcb[:, cs], wk_ref[h].astype(BF16))
        vb = v_ref[:, :, cs].reshape(L, dh).astype(BF16)
        b_col = bcum[:, H + h:H + h + 1]
        i_col = g[:, h:h + 1]
        b_row = bcum_t[H + h:H + h + 1, :]
        i_row = g_t[h:h + 1, :]
        m_prev = m_scr[h, 0:1, 0:1]
        c_prev = c_scr[h]
        n_prev = n_scr[h, 0:1, :]

        dmat = jnp.where(causal, b_col - b_row + i_row, -jnp.inf)
        inter = b_col + m_prev
        m_t = jnp.maximum(inter, jnp.max(dmat, axis=-1, keepdims=True))
        w_inter = jnp.exp(inter - m_t)
        s = _dot_nt(q, kf.astype(BF16)) * jnp.exp(dmat - m_t)
        num = w_inter * _dot(q, c_prev.astype(BF16)) + _dot(s.astype(BF16), vb)
        qn = jnp.sum(q.astype(F32) * n_prev, axis=-1, keepdims=True)
        den = w_inter * qn + jnp.sum(s, axis=-1, keepdims=True)
        ht = num / jnp.maximum(jnp.abs(den), jnp.exp(-m_t))

        g_tot = b_row[:, L - 1:L]
        dec_row = g_tot - b_row + i_row
        m_new = jnp.maximum(g_tot + m_prev, jnp.max(dec_row, axis=-1, keepdims=True))
        w_old = jnp.exp(g_tot + m_prev - m_new)
        kw = kf * jnp.exp(g_tot - b_col + i_col - m_new)
        c_scr[h] = w_old * c_prev + _dot_tn(kw.astype(BF16), vb)
        n_scr[h] = jnp.broadcast_to(w_old * n_prev + jnp.sum(kw, axis=0, keepdims=True), n_scr.shape[1:])
        m_scr[h] = jnp.broadcast_to(m_new, m_scr.shape[1:])

        hc = jax.nn.sigmoid(o_ref[:, :, cs].reshape(L, dh)) * ht
        hn = hc * lax.rsqrt(jnp.mean(hc * hc, axis=-1, keepdims=True) + NORM_EPS)
        y_ref[:, :, cs] = (hn * ng_ref[:, cs] + sk_ref[:, cs] * xc[:, cs]).reshape(T, CS, dh)


def _mlstm(xml, v, o, ifg, conv_w, conv_b, wq, wk, norm_g, skip, layer, batch):
    T, B, ncb, W = xml.shape
    H, dh = wq.shape[1], wq.shape[2]
    K = conv_w.shape[1]
    CS = ML_CHUNK_S5

    def seq(w):
        return pl.BlockSpec((T, None, CS, w), lambda b, c: (0, b, c, 0))

    return pl.pallas_call(
        _mlstm_kernel,
        grid=(B, ncb // CS),
        in_specs=[seq(W), seq(W), seq(W), seq(GATE_LANES),
                  _const_spec((K, W), layer), _const_spec((1, W), layer),
                  _const_spec((H, dh, dh), layer), _const_spec((H, dh, dh), layer),
                  _const_spec((1, W), layer), _const_spec((1, W), layer)],
        out_specs=seq(W),
        out_shape=jax.ShapeDtypeStruct((T, B, ncb, W), F32),
        scratch_shapes=[pltpu.VMEM((H, dh, dh), F32), pltpu.VMEM((H, 8, dh), F32),
                        pltpu.VMEM((H, 8, 128), F32), pltpu.VMEM(((K - 1) * CS, W), F32)],
        compiler_params=_params(("parallel", "arbitrary")),
        name="mlstm",
    )(xml, v, o, ifg, conv_w, conv_b, wq, wk, norm_g, skip)


def _gelu_tanh(x):
    return 0.5 * x * (1.0 + jnp.tanh(math.sqrt(2.0 / math.pi) * (x + 0.044715 * (x * x * x))))


def _merge_kernel(x_ref, yt_ref, ut_ref, yml_ref, g_ref, wgate_ref, d_ref, gv_ref, gg_ref,
                  wbs_ref, wbm_ref, wout_ref, o_ref):
    x = x_ref[...]
    d = x.shape[1]
    h = _rms(x, g_ref[...]).astype(BF16)
    z = _gelu_tanh(yt_ref[...] + d_ref[...] * ut_ref[...]).T.astype(BF16)
    ys5 = _dot(z, gv_ref[...].astype(BF16)) * jax.nn.sigmoid(_dot(z, gg_ref[...].astype(BF16)))
    gate_s5 = jax.nn.sigmoid(_dot(h, wgate_ref[:, :d]))
    mix = gate_s5 * _dot(ys5.astype(BF16), wbs_ref[...].astype(BF16))
    gate_ml = jax.nn.sigmoid(_dot(h, wgate_ref[:, d:]))
    mix = mix + gate_ml * _dot(yml_ref[...].astype(BF16), wbm_ref[...].astype(BF16))
    o_ref[...] = x + _dot(mix.astype(BF16), wout_ref[...].astype(BF16))


def _merge(x, yt, ut, yml, norm, w_gate, d_skip, gv, gg, wbs, wbm, wout, layer):
    n, d = x.shape
    T, s5w, nc = yt.shape
    mlw = yml.shape[1]

    def tok(w):
        return pl.BlockSpec((nc, w), lambda i: (i, 0))

    slab = pl.BlockSpec((None, s5w, nc), lambda i: (i, 0, 0))
    return pl.pallas_call(
        _merge_kernel,
        grid=(T,),
        in_specs=[tok(d), slab, slab, tok(mlw),
                  _const_spec((1, d), layer), _const_spec((d, 2 * d), layer),
                  _const_spec((s5w, 1), layer), _const_spec((s5w, s5w), layer),
                  _const_spec((s5w, s5w), layer), _const_spec((s5w, d), layer),
                  _const_spec((mlw, d), layer), _const_spec((d, d), layer)],
        out_specs=tok(d),
        out_shape=jax.ShapeDtypeStruct((n, d), F32),
        compiler_params=_params(("parallel",)),
        name="merge",
    )(x, yt, ut, yml, norm, w_gate, d_skip, gv, gg, wbs, wbm, wout)


def kernel(x, ffn1_norm, ffn1_wg, ffn1_wu, ffn1_wd, mix_norm, w_in, b_if, s5_lam_re, s5_lam_im, s5_log_dt, s5_b_re, s5_b_im, s5_c_re, s5_c_im, s5_d, s5_glu_v, s5_glu_g, ml_conv_w, ml_conv_b, ml_wq, ml_wk, ml_norm, ml_skip, w_br_s5, w_br_ml, w_out, ffn2_norm, ffn2_wg, ffn2_wu, ffn2_wd, final_norm):
    B, L, D = x.shape
    depth = w_in.shape[0]
    s5w = s5_d.shape[1]
    mlw = ml_norm.shape[1]
    H = ml_wq.shape[1]
    T = S5_CHUNK
    ncb = L // T
    nc = B * ncb
    assert ncb % S5_SEGMENTS == 0 and ncb % ML_CHUNK_S5 == 0 and (T * nc) % TOKEN_TILE == 0
    o3 = s5w + 3 * mlw
    o4 = o3 + 2 * H

    bf = lambda a: a.astype(BF16)
    row = lambda a: a[:, None, :]
    ffn1 = (row(ffn1_norm), ffn1_wg, ffn1_wu, ffn1_wd)
    ffn2 = (row(ffn2_norm), ffn2_wg, ffn2_wu, ffn2_wd)
    b_ifp = row(jnp.pad(b_if, ((0, 0), (0, GATE_LANES - 2 * H))))
    w_gate = bf(w_in[:, :, o4:])
    mixn, conv_b, ml_n, ml_s = row(mix_norm), row(ml_conv_b), row(ml_norm), row(ml_skip)
    d_skip = s5_d[:, :, None]
    fin = final_norm[None, :]
    tables = [_s5_tables(s5_lam_re[l], s5_lam_im[l], s5_log_dt[l], s5_b_re[l], s5_b_im[l],
                         s5_c_re[l], s5_c_im[l], ncb // S5_SEGMENTS) for l in range(depth)]

    xt = x.reshape(B, ncb, T, D).transpose(2, 0, 1, 3).reshape(T * nc, D)
    seq4 = lambda a: a.reshape(T, B, ncb, a.shape[-1])
    for l in range(depth):
        xt = _ffn(xt, *ffn1, fin, l, False)
        ut, xml, v, o, ifg = _inproj(xt, mixn, w_in, b_ifp, l, nc, s5w, mlw, 2 * H)
        yt = _s5_scan(ut, *tables[l], B)
        yml = _mlstm(seq4(xml), seq4(v), seq4(o), seq4(ifg), ml_conv_w, conv_b, ml_wq, ml_wk, ml_n, ml_s, l, B)
        xt = _merge(xt, yt, ut, yml.reshape(T * nc, mlw), mixn, w_gate, d_skip, s5_glu_v, s5_glu_g,
                    w_br_s5, w_br_ml, w_out, l)
        xt = _ffn(xt, *ffn2, fin, l, l == depth - 1)
    return xt.reshape(T, B, ncb, D).transpose(1, 2, 0, 3).reshape(B, L, D)
```

```python
import functools
import math

import jax
import jax.numpy as jnp
from jax import lax
from jax.experimental import pallas as pl
from jax.experimental.pallas import tpu as pltpu

F32 = jnp.float32
BF16 = jnp.bfloat16

NORM_EPS = 1e-6
S5_GROUP = 16
S5_CHUNK = 32
S5_SEGMENTS = 8
ML_HEADS = 4
ML_CHUNK_S5 = 8
GATE_LANES = 128
TOKEN_TILE = 512
FF_CHUNK = 256
VMEM_LIMIT = 56 * 1024 * 1024
HIGHEST = lax.Precision.HIGHEST


def _rms(x, g):
    return x * lax.rsqrt(jnp.mean(x * x, axis=-1, keepdims=True) + NORM_EPS) * g


def _dot(a, b):
    return jnp.dot(a, b, preferred_element_type=F32)


def _dot_nt(a, b):
    return lax.dot_general(a, b, (((1,), (1,)), ((), ())), preferred_element_type=F32)


def _dot_tn(a, b):
    return lax.dot_general(a, b, (((0,), (0,)), ((), ())), preferred_element_type=F32)


def _const_spec(shape, layer=None):
    nd = len(shape)
    if layer is None:
        return pl.BlockSpec(shape, lambda *_: (0,) * nd, pipeline_mode=pl.Buffered(1))
    return pl.BlockSpec((None,) + tuple(shape), lambda *_: (layer,) + (0,) * nd,
                        pipeline_mode=pl.Buffered(1))


def _params(sem):
    return pltpu.CompilerParams(dimension_semantics=sem, vmem_limit_bytes=VMEM_LIMIT)


def _ffn_kernel(x_ref, g_ref, wg_ref, wu_ref, wd_ref, gf_ref, o_ref, *, final):
    x = x_ref[...]
    h = _rms(x, g_ref[...]).astype(BF16)
    d_ff = wg_ref.shape[1]
    acc = jnp.zeros(x.shape, F32)
    for c in range(d_ff // FF_CHUNK):
        sl = slice(c * FF_CHUNK, (c + 1) * FF_CHUNK)
        g = _dot(h, wg_ref[:, sl].astype(BF16))
        u = _dot(h, wu_ref[:, sl].astype(BF16))
        a = (g * jax.nn.sigmoid(g) * u).astype(BF16)
        acc = acc + _dot(a, wd_ref[sl, :].astype(BF16))
    y = x + 0.5 * acc
    if final:
        y = _rms(y, gf_ref[...])
    o_ref[...] = y


def _ffn(x, norm, wg, wu, wd, final_norm, layer, final):
    n, d = x.shape
    d_ff = wg.shape[2]
    tok = pl.BlockSpec((TOKEN_TILE, d), lambda i: (i, 0))
    return pl.pallas_call(
        functools.partial(_ffn_kernel, final=final),
        grid=(n // TOKEN_TILE,),
        in_specs=[tok, _const_spec((1, d), layer), _const_spec((d, d_ff), layer),
                  _const_spec((d, d_ff), layer), _const_spec((d_ff, d), layer),
                  _const_spec((1, d))],
        out_specs=tok,
        out_shape=jax.ShapeDtypeStruct((n, d), F32),
        compiler_params=_params(("parallel",)),
        name="ffn",
    )(x, norm, wg, wu, wd, final_norm)


def _inproj_kernel(x_ref, g_ref, w_ref, bif_ref, ut_ref, xml_ref, v_ref, o_ref, if_ref, *, n_gates):
    h = _rms(x_ref[...], g_ref[...]).astype(BF16)
    c0 = ut_ref.shape[0]
    c1 = c0 + xml_ref.shape[1]
    c2 = c1 + v_ref.shape[1]
    c3 = c2 + o_ref.shape[1]
    gl = if_ref.shape[1]
    ut_ref[...] = _dot(h, w_ref[:, :c0].astype(BF16)).T
    xml_ref[...] = _dot(h, w_ref[:, c0:c1].astype(BF16))
    v_ref[...] = _dot(h, w_ref[:, c1:c2].astype(BF16))
    o_ref[...] = _dot(h, w_ref[:, c2:c3].astype(BF16))
    gates = _dot(h, w_ref[:, c3:c3 + gl].astype(BF16))
    lane = lax.broadcasted_iota(jnp.int32, gates.shape, 1)
    if_ref[...] = jnp.where(lane < n_gates, gates, 0.0) + bif_ref[...]


def _inproj(x, norm, w_in, b_if, layer, nc, s5w, mlw, n_gates):
    n, d = x.shape
    cols = w_in.shape[2]
    assert (s5w + 3 * mlw) % GATE_LANES == 0 and s5w + 3 * mlw + GATE_LANES <= cols

    def tok(w):
        return pl.BlockSpec((nc, w), lambda i: (i, 0))

    widths = (mlw, mlw, mlw, GATE_LANES)
    return pl.pallas_call(
        functools.partial(_inproj_kernel, n_gates=n_gates),
        grid=(n // nc,),
        in_specs=[tok(d), _const_spec((1, d), layer), _const_spec((d, cols), layer),
                  _const_spec((1, GATE_LANES), layer)],
        out_specs=[pl.BlockSpec((None, s5w, nc), lambda i: (i, 0, 0))] + [tok(w) for w in widths],
        out_shape=[jax.ShapeDtypeStruct((n // nc, s5w, nc), F32)]
        + [jax.ShapeDtypeStruct((n, w), F32) for w in widths],
        compiler_params=_params(("parallel",)),
        name="inproj",
    )(x, norm, w_in, b_if)


def _s5_pack(lam_re, lam_im, log_dt, b_re, b_im, c_re, c_im):
    two = lambda a: jnp.concatenate([a, a], axis=-1)
    lam = jnp.stack([two(lam_re), two(lam_im), jnp.broadcast_to(log_dt[..., None], two(lam_re).shape)]
                    + [jnp.zeros_like(two(lam_re))] * 5, axis=-2)
    bmat = jnp.concatenate([b_re, b_im], axis=-2)
    cc = jnp.concatenate([two(c_re), two(c_im)], axis=-2)
    return lam, bmat, cc


def _s5_operators(lam, b1, cc, T, P, R):
    n2 = lam.shape[1]
    N = n2 // 2
    tp = T * P
    lr = jnp.minimum(lam[0:1, :], -1e-4)
    li = lam[1:2, :]
    dt = jnp.exp(lam[2:3, :])
    lrdt = lr * dt
    lidt = li * dt
    half = lax.broadcasted_iota(jnp.int32, (1, n2), 1) < N

    def powers(k):
        pm = jnp.exp(lrdt * k)
        return pm * jnp.cos(lidt * k), pm * jnp.sin(lidt * k)

    kk = lax.broadcasted_iota(jnp.int32, (T + 8, 1), 0).astype(F32)
    pre, pim = powers(kk)
    rr = lax.broadcasted_iota(jnp.int32, (R + 8, 1), 0).astype(F32) * float(T)
    sre, sim = powers(rr)

    def forms(re, im):
        f2 = jnp.where(half, -im, im)
        return re, f2, -f2

    def to_columns(rows):
        pad = jnp.zeros((n2 - rows.shape[0], n2), F32)
        return jnp.concatenate([rows, pad], axis=0).T

    ab_re, ab_im = pre[1:2, :], pim[1:2, :]
    den = lr * lr + li * li
    q_re = ((ab_re - 1.0) * lr + ab_im * li) / den
    q_im = (ab_im * lr - (ab_re - 1.0) * li) / den
    q_col = to_columns(jnp.concatenate([q_re, q_im], axis=0))
    q_re_c, q_im_c = q_col[:, 0:1], q_col[:, 1:2]
    b2 = jnp.concatenate([-b1[N:, :], b1[:N, :]], axis=0)
    bbs = q_re_c * b1 + q_im_c * b2
    bbx = q_re_c * b2 - q_im_c * b1

    cc1, cc2 = cc[0:P, :], cc[P:2 * P, :]
    pp1 = jnp.where(half, pre, -pim)
    pp2 = jnp.where(half, -pim, -pre)

    def readout(k0):
        return jnp.concatenate([pp1[k:k + 1, :] * cc1 + pp2[k:k + 1, :] * cc2 for k in range(k0, k0 + T)], axis=0)

    v = readout(1).astype(BF16)
    kcol = jnp.dot(readout(0), bbs, preferred_element_type=F32, precision=HIGHEST).astype(BF16)

    col_i = lax.broadcasted_iota(jnp.int32, (n2, tp), 1) // P
    pick_i = jnp.where(col_i == T - 1 - lax.broadcasted_iota(jnp.int32, (n2, tp), 0), 1.0, 0.0).astype(BF16)
    col_p = lax.broadcasted_iota(jnp.int32, (P, tp), 1) % P
    pick_p = jnp.where(col_p == lax.broadcasted_iota(jnp.int32, (P, tp), 0), 1.0, 0.0).astype(BF16)

    def expand(a, e):
        hi = a.astype(BF16)
        return _dot(hi, e) + _dot((a - hi.astype(F32)).astype(BF16), e)

    bt1 = expand(bbs, pick_p)
    bt2 = jnp.concatenate([-bt1[N:, :], bt1[:N, :]], axis=0)
    w_s = expand(to_columns(pre[0:T, :]), pick_i) * bt1 + expand(to_columns(pim[0:T, :]), pick_i) * bt2
    w = jnp.concatenate([w_s, w_s[N:, :], w_s[:N, :]], axis=0).astype(BF16)

    step = forms(pre[T:T + 1, :], pim[T:T + 1, :])
    seg = forms(sre[R:R + 1, :], sim[R:R + 1, :])
    pw1, pw2, _ = forms(sre[0:R, :], sim[0:R, :])
    return kcol, w, v, step, seg, pw1, pw2


def _toeplitz(kcol, T, P):
    tp = T * P
    lane = lax.broadcasted_iota(jnp.int32, (P, tp), 1)
    sub = lax.broadcasted_iota(jnp.int32, (P, tp), 0)
    rep = jnp.where(lane % P == sub, 1.0, 0.0).astype(BF16)
    m = _dot(kcol, rep)
    blk = lax.broadcasted_iota(jnp.int32, (tp, tp), 1) // P
    shift = P
    while shift < tp:
        moved = jnp.concatenate([jnp.zeros((shift, tp), F32), m[:tp - shift, :]], axis=0)
        m = jnp.where((blk & (shift // P)) != 0, moved, m)
        shift *= 2
    return m.astype(BF16)


def _s5_kernel(u_ref, lam_ref, b_ref, cc_ref, y_ref, *, batch):
    T, P, nct = u_ref.shape
    n2 = lam_ref.shape[1]
    S = S5_SEGMENTS
    ncb = nct // batch
    R = ncb // S
    kcol, w, v, (c1, c2, c2t), (g1, g2, g2t), pw1, pw2 = _s5_operators(
        lam_ref[...], b_ref[...], cc_ref[...], T, P, R)
    u = u_ref[...].reshape(T * P, nct).astype(BF16)
    m = _toeplitz(kcol, T, P)
    x = _dot(w, u).T
    xr = [jnp.swapaxes(x[b * ncb:(b + 1) * ncb, :].reshape(S, R, 2 * n2), 0, 1).reshape(ncb, 2 * n2)
          for b in range(batch)]
    rows = lax.broadcasted_iota(jnp.int32, (S, n2), 0)

    s = [jnp.zeros((S, n2), F32)] * batch
    t = [jnp.zeros((S, n2), F32)] * batch
    local = [[] for _ in range(batch)]
    for r in range(R):
        for b in range(batch):
            local[b].append(s[b])
            xin = xr[b][r * S:(r + 1) * S, :]
            s[b], t[b] = c1 * s[b] + c2 * t[b] + xin[:, :n2], c1 * t[b] + c2t * s[b] + xin[:, n2:]
    ini_s = [jnp.zeros((S, n2), F32)] * batch
    ini_t = [jnp.zeros((S, n2), F32)] * batch
    cur_s = [jnp.zeros((1, n2), F32)] * batch
    cur_t = [jnp.zeros((1, n2), F32)] * batch
    for k in range(1, S):
        for b in range(batch):
            cur_s[b], cur_t[b] = (g1 * cur_s[b] + g2 * cur_t[b] + s[b][k - 1:k, :],
                                  g1 * cur_t[b] + g2t * cur_s[b] + t[b][k - 1:k, :])
            ini_s[b] = jnp.where(rows == k, cur_s[b], ini_s[b])
            ini_t[b] = jnp.where(rows == k, cur_t[b], ini_t[b])
    before = []
    for b in range(batch):
        sb = jnp.concatenate([local[b][r] + pw1[r:r + 1, :] * ini_s[b] + pw2[r:r + 1, :] * ini_t[b]
                              for r in range(R)], axis=0)
        before.append(jnp.swapaxes(sb.reshape(R, S, n2), 0, 1).reshape(ncb, n2))
    y = _dot(m, u) + _dot_nt(v, jnp.concatenate(before, axis=0).astype(BF16))
    y_ref[...] = y.reshape(T, P, nct)


def _s5_scan(ut, lam, bmat, cc, layer, batch):
    T, s5w, nct = ut.shape
    G = s5w // S5_GROUP
    n2 = lam.shape[-1]

    def grp(a):
        return pl.BlockSpec((None, None) + a.shape[2:], lambda g: (layer, g, 0, 0))

    slab = pl.BlockSpec((T, S5_GROUP, nct), lambda g: (0, g, 0))
    return pl.pallas_call(
        functools.partial(_s5_kernel, batch=batch),
        grid=(G,),
        in_specs=[slab, grp(lam), grp(bmat), grp(cc)],
        out_specs=slab,
        out_shape=jax.ShapeDtypeStruct((T, s5w, nct), F32),
        compiler_params=_params(("parallel",)),
        name="s5_scan",
    )(ut, lam, bmat, cc)


def _mlstm_kernel(xml_ref, v_ref, o_ref, if_ref, cw_ref, cb_ref, wq_ref, wk_ref, ng_ref, sk_ref,
                  y_ref, c_scr, n_scr, m_scr, tail_scr):
    H = wq_ref.shape[0]
    dh = wq_ref.shape[1]
    T, CS, W = xml_ref.shape
    L = T * CS
    K = cw_ref.shape[0]

    @pl.when(pl.program_id(1) == 0)
    def _():
        c_scr[...] = jnp.zeros(c_scr.shape, F32)
        n_scr[...] = jnp.zeros(n_scr.shape, F32)
        m_scr[...] = jnp.zeros(m_scr.shape, F32)
        tail_scr[...] = jnp.zeros(tail_scr.shape, F32)

    x = xml_ref[...].reshape(L, W)
    nw = (K - 1) * CS
    last = x[L - nw:, :]
    prev = tail_scr[...]
    crow = lax.broadcasted_iota(jnp.int32, (nw, W), 0) % CS
    wrap = jnp.where(crow == 0, pltpu.roll(prev, nw - (CS - 1), 0), pltpu.roll(last, 1, 0))
    tail_scr[...] = last
    conv = cb_ref[...] + cw_ref[K - 1:K, :] * x
    for d in range(1, K):
        xd = jnp.concatenate([wrap[nw - d * CS:, :], x[:L - d * CS, :]], axis=0)
        conv = conv + cw_ref[K - 1 - d:K - d, :] * xd
    xc = conv * jax.nn.sigmoid(conv)
    xcb = xc.astype(BF16)

    g = if_ref[...].reshape(L, if_ref.shape[2])
    lf = jnp.minimum(g, 0.0) - jnp.log1p(jnp.exp(-jnp.abs(g)))
    ri = lax.broadcasted_iota(jnp.int32, (L, L), 0)
    ci = lax.broadcasted_iota(jnp.int32, (L, L), 1)
    causal = ((ci % CS) * T + ci // CS) <= ((ri % CS) * T + ri // CS)
    bcum = jnp.dot(causal.astype(F32), lf, preferred_element_type=F32, precision=HIGHEST)
    bcum_t = bcum.T
    g_t = g.T

    for h in range(H):
        cs = slice(h * dh, (h + 1) * dh)
        q = (_dot(xcb[:, cs], wq_ref[h].astype(BF16)) * (dh ** -0.5)).astype(BF16)
        kf = _dot(xcb[:, cs], wk_ref[h].astype(BF16))
        vb = v_ref[:, :, cs].reshape(L, dh).astype(BF16)
        b_col = bcum[:, H + h:H + h + 1]
        i_col = g[:, h:h + 1]
        b_row = bcum_t[H + h:H + h + 1, :]
        i_row = g_t[h:h + 1, :]
        m_prev = m_scr[h, 0:1, 0:1]
        c_prev = c_scr[h]
        n_prev = n_scr[h, 0:1, :]

        dmat = jnp.where(causal, b_col - b_row + i_row, -jnp.inf)
        inter = b_col + m_prev
        m_t = jnp.maximum(inter, jnp.max(dmat, axis=-1, keepdims=True))
        w_inter = jnp.exp(inter - m_t)
        s = _dot_nt(q, kf.astype(BF16)) * jnp.exp(dmat - m_t)
        num = w_inter * _dot(q, c_prev.astype(BF16)) + _dot(s.astype(BF16), vb)
        qn = jnp.sum(q.astype(F32) * n_prev, axis=-1, keepdims=True)
        den = w_inter * qn + jnp.sum(s, axis=-1, keepdims=True)
        ht = num / jnp.maximum(jnp.abs(den), jnp.exp(-m_t))

        g_tot = b_row[:, L - 1:L]
        dec_row = g_tot - b_row + i_row
        m_new = jnp.maximum(g_tot + m_prev, jnp.max(dec_row, axis=-1, keepdims=True))
        w_old = jnp.exp(g_tot + m_prev - m_new)
        kw = kf * jnp.exp(g_tot - b_col + i_col - m_new)
        c_scr[h] = w_old * c_prev + _dot_tn(kw.astype(BF16), vb)
        n_scr[h] = jnp.broadcast_to(w_old * n_prev + jnp.sum(kw, axis=0, keepdims=True), n_scr.shape[1:])
        m_scr[h] = jnp.broadcast_to(m_new, m_scr.shape[1:])

        hc = jax.nn.sigmoid(o_ref[:, :, cs].reshape(L, dh)) * ht
        hn = hc * lax.rsqrt(jnp.mean(hc * hc, axis=-1, keepdims=True) + NORM_EPS)
        y_ref[:, :, cs] = (hn * ng_ref[:, cs] + sk_ref[:, cs] * xc[:, cs]).reshape(T, CS, dh)


def _mlstm(xml, v, o, ifg, conv_w, conv_b, wq, wk, norm_g, skip, layer, batch):
    T, B, ncb, W = xml.shape
    H, dh = wq.shape[1], wq.shape[2]
    K = conv_w.shape[1]
    CS = ML_CHUNK_S5

    def seq(w):
        return pl.BlockSpec((T, None, CS, w), lambda b, c: (0, b, c, 0))

    return pl.pallas_call(
        _mlstm_kernel,
        grid=(B, ncb // CS),
        in_specs=[seq(W), seq(W), seq(W), seq(GATE_LANES),
                  _const_spec((K, W), layer), _const_spec((1, W), layer),
                  _const_spec((H, dh, dh), layer), _const_spec((H, dh, dh), layer),
                  _const_spec((1, W), layer), _const_spec((1, W), layer)],
        out_specs=seq(W),
        out_shape=jax.ShapeDtypeStruct((T, B, ncb, W), F32),
        scratch_shapes=[pltpu.VMEM((H, dh, dh), F32), pltpu.VMEM((H, 8, dh), F32),
                        pltpu.VMEM((H, 8, 128), F32), pltpu.VMEM(((K - 1) * CS, W), F32)],
        compiler_params=_params(("parallel", "arbitrary")),
        name="mlstm",
    )(xml, v, o, ifg, conv_w, conv_b, wq, wk, norm_g, skip)


def _gelu_tanh(x):
    return 0.5 * x * (1.0 + jnp.tanh(math.sqrt(2.0 / math.pi) * (x + 0.044715 * (x * x * x))))


def _merge_kernel(x_ref, yt_ref, ut_ref, yml_ref, g_ref, wgate_ref, d_ref, gv_ref, gg_ref,
                  wbs_ref, wbm_ref, wout_ref, o_ref):
    x = x_ref[...]
    d = x.shape[1]
    h = _rms(x, g_ref[...]).astype(BF16)
    z = _gelu_tanh(yt_ref[...] + d_ref[...] * ut_ref[...]).T.astype(BF16)
    ys5 = _dot(z, gv_ref[...].astype(BF16)) * jax.nn.sigmoid(_dot(z, gg_ref[...].astype(BF16)))
    gate_s5 = jax.nn.sigmoid(_dot(h, wgate_ref[:, :d]))
    mix = gate_s5 * _dot(ys5.astype(BF16), wbs_ref[...].astype(BF16))
    gate_ml = jax.nn.sigmoid(_dot(h, wgate_ref[:, d:]))
    mix = mix + gate_ml * _dot(yml_ref[...].astype(BF16), wbm_ref[...].astype(BF16))
    o_ref[...] = x + _dot(mix.astype(BF16), wout_ref[...].astype(BF16))


def _merge(x, yt, ut, yml, norm, w_gate, d_skip, gv, gg, wbs, wbm, wout, layer):
    n, d = x.shape
    T, s5w, nc = yt.shape
    mlw = yml.shape[1]

    def tok(w):
        return pl.BlockSpec((nc, w), lambda i: (i, 0))

    slab = pl.BlockSpec((None, s5w, nc), lambda i: (i, 0, 0))
    return pl.pallas_call(
        _merge_kernel,
        grid=(T,),
        in_specs=[tok(d), slab, slab, tok(mlw),
                  _const_spec((1, d), layer), _const_spec((d, 2 * d), layer),
                  _const_spec((s5w, 1), layer), _const_spec((s5w, s5w), layer),
                  _const_spec((s5w, s5w), layer), _const_spec((s5w, d), layer),
                  _const_spec((mlw, d), layer), _const_spec((d, d), layer)],
        out_specs=tok(d),
        out_shape=jax.ShapeDtypeStruct((n, d), F32),
        compiler_params=_params(("parallel",)),
        name="merge",
    )(x, yt, ut, yml, norm, w_gate, d_skip, gv, gg, wbs, wbm, wout)


def _gate_weight_kernel(w_ref, o_ref, *, start):
    o_ref[...] = w_ref[:, start:start + o_ref.shape[1]].astype(BF16)


def _gate_weights(w_in, start, width):
    depth, d, cols = w_in.shape
    return pl.pallas_call(
        functools.partial(_gate_weight_kernel, start=start),
        grid=(depth,),
        in_specs=[pl.BlockSpec((None, d, cols), lambda l: (l, 0, 0), pipeline_mode=pl.Buffered(1))],
        out_specs=pl.BlockSpec((None, d, width), lambda l: (l, 0, 0)),
        out_shape=jax.ShapeDtypeStruct((depth, d, width), BF16),
        compiler_params=_params(("parallel",)),
        name="gate_weights",
    )(w_in)


def kernel(x, ffn1_norm, ffn1_wg, ffn1_wu, ffn1_wd, mix_norm, w_in, b_if, s5_lam_re, s5_lam_im, s5_log_dt, s5_b_re, s5_b_im, s5_c_re, s5_c_im, s5_d, s5_glu_v, s5_glu_g, ml_conv_w, ml_conv_b, ml_wq, ml_wk, ml_norm, ml_skip, w_br_s5, w_br_ml, w_out, ffn2_norm, ffn2_wg, ffn2_wu, ffn2_wd, final_norm):
    B, L, D = x.shape
    depth = w_in.shape[0]
    s5w = s5_d.shape[1]
    mlw = ml_norm.shape[1]
    H = ml_wq.shape[1]
    T = S5_CHUNK
    ncb = L // T
    nc = B * ncb
    assert ncb % S5_SEGMENTS == 0 and ncb % ML_CHUNK_S5 == 0 and (T * nc) % TOKEN_TILE == 0
    o3 = s5w + 3 * mlw
    o4 = o3 + 2 * H

    row = lambda a: a[:, None, :]
    ffn1 = (row(ffn1_norm), ffn1_wg, ffn1_wu, ffn1_wd)
    ffn2 = (row(ffn2_norm), ffn2_wg, ffn2_wu, ffn2_wd)
    b_ifp = row(jnp.pad(b_if, ((0, 0), (0, GATE_LANES - 2 * H))))
    w_gate = _gate_weights(w_in, o4, 2 * D)
    mixn, conv_b, ml_n, ml_s = row(mix_norm), row(ml_conv_b), row(ml_norm), row(ml_skip)
    d_skip = s5_d[:, :, None]
    fin = final_norm[None, :]
    s5_params = _s5_pack(s5_lam_re, s5_lam_im, s5_log_dt, s5_b_re, s5_b_im, s5_c_re, s5_c_im)

    xt = x.reshape(B, ncb, T, D).transpose(2, 0, 1, 3).reshape(T * nc, D)
    seq4 = lambda a: a.reshape(T, B, ncb, a.shape[-1])
    for l in range(depth):
        xt = _ffn(xt, *ffn1, fin, l, False)
        ut, xml, v, o, ifg = _inproj(xt, mixn, w_in, b_ifp, l, nc, s5w, mlw, 2 * H)
        yt = _s5_scan(ut, *s5_params, l, B)
        yml = _mlstm(seq4(xml), seq4(v), seq4(o), seq4(ifg), ml_conv_w, conv_b, ml_wq, ml_wk, ml_n, ml_s, l, B)
        xt = _merge(xt, yt, ut, yml.reshape(T * nc, mlw), mixn, w_gate, d_skip, s5_glu_v, s5_glu_g,
                    w_br_s5, w_br_ml, w_out, l)
        xt = _ffn(xt, *ffn2, fin, l, l == depth - 1)
    return xt.reshape(T, B, ncb, D).transpose(1, 2, 0, 3).reshape(B, L, D)
```

```python
import functools
import math

import jax
import jax.numpy as jnp
from jax import lax
from jax.experimental import pallas as pl
from jax.experimental.pallas import tpu as pltpu

F32 = jnp.float32
BF16 = jnp.bfloat16

NORM_EPS = 1e-6
S5_GROUP = 16
S5_CHUNK = 32
S5_SEGMENTS = 8
ML_HEADS = 4
ML_CHUNK_S5 = 8
GATE_LANES = 128
TOKEN_TILE = 512
FF_CHUNK = 256
MERGE_SPLIT = 2
VMEM_LIMIT = 56 * 1024 * 1024
HIGHEST = lax.Precision.HIGHEST


def _rms(x, g):
    return x * lax.rsqrt(jnp.mean(x * x, axis=-1, keepdims=True) + NORM_EPS) * g


def _dot(a, b):
    return jnp.dot(a, b, preferred_element_type=F32)


def _dot_nt(a, b):
    return lax.dot_general(a, b, (((1,), (1,)), ((), ())), preferred_element_type=F32)


def _dot_tn(a, b):
    return lax.dot_general(a, b, (((0,), (0,)), ((), ())), preferred_element_type=F32)


def _const_spec(shape, layer=None):
    nd = len(shape)
    if layer is None:
        return pl.BlockSpec(shape, lambda *_: (0,) * nd, pipeline_mode=pl.Buffered(1))
    return pl.BlockSpec((None,) + tuple(shape), lambda *_: (layer,) + (0,) * nd,
                        pipeline_mode=pl.Buffered(1))


def _params(sem):
    return pltpu.CompilerParams(dimension_semantics=sem, vmem_limit_bytes=VMEM_LIMIT)


def _ffn_kernel(x_ref, g_ref, wg_ref, wu_ref, wd_ref, gf_ref, o_ref, *, final):
    x = x_ref[...]
    h = _rms(x, g_ref[...]).astype(BF16)
    d_ff = wg_ref.shape[1]
    acc = jnp.zeros(x.shape, F32)
    for c in range(d_ff // FF_CHUNK):
        sl = slice(c * FF_CHUNK, (c + 1) * FF_CHUNK)
        g = _dot(h, wg_ref[:, sl].astype(BF16))
        u = _dot(h, wu_ref[:, sl].astype(BF16))
        a = (g * jax.nn.sigmoid(g) * u).astype(BF16)
        acc = acc + _dot(a, wd_ref[sl, :].astype(BF16))
    y = x + 0.5 * acc
    if final:
        y = _rms(y, gf_ref[...])
    o_ref[...] = y


def _ffn(x, norm, wg, wu, wd, final_norm, layer, final):
    n, d = x.shape
    d_ff = wg.shape[2]
    tok = pl.BlockSpec((TOKEN_TILE, d), lambda i: (i, 0))
    return pl.pallas_call(
        functools.partial(_ffn_kernel, final=final),
        grid=(n // TOKEN_TILE,),
        in_specs=[tok, _const_spec((1, d), layer), _const_spec((d, d_ff), layer),
                  _const_spec((d, d_ff), layer), _const_spec((d_ff, d), layer),
                  _const_spec((1, d))],
        out_specs=tok,
        out_shape=jax.ShapeDtypeStruct((n, d), F32),
        compiler_params=_params(("parallel",)),
        name="ffn",
    )(x, norm, wg, wu, wd, final_norm)


def _inproj_kernel(x_ref, g_ref, w_ref, bif_ref, ut_ref, xml_ref, v_ref, o_ref, if_ref, *, n_gates):
    h = _rms(x_ref[...], g_ref[...]).astype(BF16)
    c0 = ut_ref.shape[0]
    c1 = c0 + xml_ref.shape[1]
    c2 = c1 + v_ref.shape[1]
    c3 = c2 + o_ref.shape[1]
    gl = if_ref.shape[1]
    ut_ref[...] = _dot_nt(w_ref[:c0, :].astype(BF16), h)
    xml_ref[...] = _dot_nt(h, w_ref[c0:c1, :].astype(BF16))
    v_ref[...] = _dot_nt(h, w_ref[c1:c2, :].astype(BF16))
    o_ref[...] = _dot_nt(h, w_ref[c2:c3, :].astype(BF16))
    gates = _dot_nt(h, w_ref[c3:c3 + gl, :].astype(BF16))
    lane = lax.broadcasted_iota(jnp.int32, gates.shape, 1)
    if_ref[...] = jnp.where(lane < n_gates, gates, 0.0) + bif_ref[...]


def _inproj(x, norm, w_in, b_if, layer, nc, s5w, mlw, n_gates):
    n, d = x.shape
    cols = w_in.shape[1]
    assert s5w + 3 * mlw + GATE_LANES <= cols

    def tok(w):
        return pl.BlockSpec((nc, w), lambda i: (i, 0))

    widths = (mlw, mlw, mlw, GATE_LANES)
    return pl.pallas_call(
        functools.partial(_inproj_kernel, n_gates=n_gates),
        grid=(n // nc,),
        in_specs=[tok(d), _const_spec((1, d), layer), _const_spec((cols, d), layer),
                  _const_spec((1, GATE_LANES), layer)],
        out_specs=[pl.BlockSpec((None, s5w, nc), lambda i: (i, 0, 0))] + [tok(w) for w in widths],
        out_shape=[jax.ShapeDtypeStruct((n // nc, s5w, nc), F32)]
        + [jax.ShapeDtypeStruct((n, w), F32) for w in widths],
        compiler_params=_params(("parallel",)),
        name="inproj",
    )(x, norm, w_in, b_if)


def _s5_pack(lam_re, lam_im, log_dt, b_re, b_im, c_re, c_im):
    two = lambda a: jnp.concatenate([a, a], axis=-1)
    lam = jnp.stack([two(lam_re), two(lam_im), jnp.broadcast_to(log_dt[..., None], two(lam_re).shape)]
                    + [jnp.zeros_like(two(lam_re))] * 5, axis=-2)
    bmat = jnp.concatenate([b_re, b_im], axis=-2)
    cc = jnp.concatenate([two(c_re), two(c_im)], axis=-2)
    return lam, bmat, cc


def _s5_operators(lam, b1, cc, T, P, R):
    n2 = lam.shape[1]
    N = n2 // 2
    tp = T * P
    lr = jnp.minimum(lam[0:1, :], -1e-4)
    li = lam[1:2, :]
    dt = jnp.exp(lam[2:3, :])
    lrdt = lr * dt
    lidt = li * dt
    half = lax.broadcasted_iota(jnp.int32, (1, n2), 1) < N

    def powers(k):
        pm = jnp.exp(lrdt * k)
        return pm * jnp.cos(lidt * k), pm * jnp.sin(lidt * k)

    kk = lax.broadcasted_iota(jnp.int32, (T + 8, 1), 0).astype(F32)
    pre, pim = powers(kk)
    rr = lax.broadcasted_iota(jnp.int32, (R + 8, 1), 0).astype(F32) * float(T)
    sre, sim = powers(rr)

    def forms(re, im):
        f2 = jnp.where(half, -im, im)
        return re, f2, -f2

    def to_columns(rows):
        pad = jnp.zeros((n2 - rows.shape[0], n2), F32)
        return jnp.concatenate([rows, pad], axis=0).T

    ab_re, ab_im = pre[1:2, :], pim[1:2, :]
    den = lr * lr + li * li
    q_re = ((ab_re - 1.0) * lr + ab_im * li) / den
    q_im = (ab_im * lr - (ab_re - 1.0) * li) / den
    q_col = to_columns(jnp.concatenate([q_re, q_im], axis=0))
    q_re_c, q_im_c = q_col[:, 0:1], q_col[:, 1:2]
    b2 = jnp.concatenate([-b1[N:, :], b1[:N, :]], axis=0)
    bbs = q_re_c * b1 + q_im_c * b2
    bbx = q_re_c * b2 - q_im_c * b1

    cc1, cc2 = cc[0:P, :], cc[P:2 * P, :]
    pp1 = jnp.where(half, pre, -pim)
    pp2 = jnp.where(half, -pim, -pre)

    def readout(k0):
        return jnp.concatenate([pp1[k:k + 1, :] * cc1 + pp2[k:k + 1, :] * cc2 for k in range(k0, k0 + T)], axis=0)

    v = readout(1).astype(BF16)
    kcol = jnp.dot(readout(0), bbs, preferred_element_type=F32, precision=HIGHEST).astype(BF16)

    col_i = lax.broadcasted_iota(jnp.int32, (n2, tp), 1) // P
    pick_i = jnp.where(col_i == T - 1 - lax.broadcasted_iota(jnp.int32, (n2, tp), 0), 1.0, 0.0).astype(BF16)
    col_p = lax.broadcasted_iota(jnp.int32, (P, tp), 1) % P
    pick_p = jnp.where(col_p == lax.broadcasted_iota(jnp.int32, (P, tp), 0), 1.0, 0.0).astype(BF16)

    def expand(a, e):
        hi = a.astype(BF16)
        return _dot(hi, e) + _dot((a - hi.astype(F32)).astype(BF16), e)

    bt1 = expand(bbs, pick_p)
    bt2 = jnp.concatenate([-bt1[N:, :], bt1[:N, :]], axis=0)
    w_s = expand(to_columns(pre[0:T, :]), pick_i) * bt1 + expand(to_columns(pim[0:T, :]), pick_i) * bt2
    w = jnp.concatenate([w_s, w_s[N:, :], w_s[:N, :]], axis=0).astype(BF16)

    step = forms(pre[T:T + 1, :], pim[T:T + 1, :])
    seg = forms(sre[R:R + 1, :], sim[R:R + 1, :])
    pw1, pw2, _ = forms(sre[0:R, :], sim[0:R, :])
    return kcol, w, v, step, seg, pw1, pw2


def _toeplitz(kcol, T, P):
    tp = T * P
    lane = lax.broadcasted_iota(jnp.int32, (P, tp), 1)
    sub = lax.broadcasted_iota(jnp.int32, (P, tp), 0)
    rep = jnp.where(lane % P == sub, 1.0, 0.0).astype(BF16)
    m = _dot(kcol, rep)
    blk = lax.broadcasted_iota(jnp.int32, (tp, tp), 1) // P
    shift = P
    while shift < tp:
        moved = jnp.concatenate([jnp.zeros((shift, tp), F32), m[:tp - shift, :]], axis=0)
        m = jnp.where((blk & (shift // P)) != 0, moved, m)
        shift *= 2
    return m.astype(BF16)


def _s5_kernel(u_ref, lam_ref, b_ref, cc_ref, y_ref, *, batch):
    T, P, nct = u_ref.shape
    n2 = lam_ref.shape[1]
    S = S5_SEGMENTS
    ncb = nct // batch
    R = ncb // S
    kcol, w, v, (c1, c2, c2t), (g1, g2, g2t), pw1, pw2 = _s5_operators(
        lam_ref[...], b_ref[...], cc_ref[...], T, P, R)
    u = u_ref[...].reshape(T * P, nct).astype(BF16)
    m = _toeplitz(kcol, T, P)
    x = _dot(w, u).T
    xr = [jnp.swapaxes(x[b * ncb:(b + 1) * ncb, :].reshape(S, R, 2 * n2), 0, 1).reshape(ncb, 2 * n2)
          for b in range(batch)]
    rows = lax.broadcasted_iota(jnp.int32, (S, n2), 0)

    s = [jnp.zeros((S, n2), F32)] * batch
    t = [jnp.zeros((S, n2), F32)] * batch
    local = [[] for _ in range(batch)]
    for r in range(R):
        for b in range(batch):
            local[b].append(s[b])
            xin = xr[b][r * S:(r + 1) * S, :]
            s[b], t[b] = c1 * s[b] + c2 * t[b] + xin[:, :n2], c1 * t[b] + c2t * s[b] + xin[:, n2:]
    ini_s = [jnp.zeros((S, n2), F32)] * batch
    ini_t = [jnp.zeros((S, n2), F32)] * batch
    cur_s = [jnp.zeros((1, n2), F32)] * batch
    cur_t = [jnp.zeros((1, n2), F32)] * batch
    for k in range(1, S):
        for b in range(batch):
            cur_s[b], cur_t[b] = (g1 * cur_s[b] + g2 * cur_t[b] + s[b][k - 1:k, :],
                                  g1 * cur_t[b] + g2t * cur_s[b] + t[b][k - 1:k, :])
            ini_s[b] = jnp.where(rows == k, cur_s[b], ini_s[b])
            ini_t[b] = jnp.where(rows == k, cur_t[b], ini_t[b])
    before = []
    for b in range(batch):
        sb = jnp.concatenate([local[b][r] + pw1[r:r + 1, :] * ini_s[b] + pw2[r:r + 1, :] * ini_t[b]
                              for r in range(R)], axis=0)
        before.append(jnp.swapaxes(sb.reshape(R, S, n2), 0, 1).reshape(ncb, n2))
    y = _dot(m, u) + _dot_nt(v, jnp.concatenate(before, axis=0).astype(BF16))
    y_ref[...] = y.reshape(T, P, nct)


def _s5_scan(ut, lam, bmat, cc, layer, batch):
    T, s5w, nct = ut.shape
    G = s5w // S5_GROUP
    n2 = lam.shape[-1]

    def grp(a):
        return pl.BlockSpec((None, None) + a.shape[2:], lambda g: (layer, g, 0, 0))

    slab = pl.BlockSpec((T, S5_GROUP, nct), lambda g: (0, g, 0))
    return pl.pallas_call(
        functools.partial(_s5_kernel, batch=batch),
        grid=(G,),
        in_specs=[slab, grp(lam), grp(bmat), grp(cc)],
        out_specs=slab,
        out_shape=jax.ShapeDtypeStruct((T, s5w, nct), F32),
        compiler_params=_params(("parallel",)),
        name="s5_scan",
    )(ut, lam, bmat, cc)


def _mlstm_kernel(xml_ref, v_ref, o_ref, if_ref, cw_ref, cb_ref, wq_ref, wk_ref, ng_ref, sk_ref,
                  y_ref, c_scr, n_scr, m_scr, tail_scr):
    H = wq_ref.shape[0]
    dh = wq_ref.shape[1]
    T, CS, W = xml_ref.shape
    L = T * CS
    K = cw_ref.shape[0]

    @pl.when(pl.program_id(1) == 0)
    def _():
        c_scr[...] = jnp.zeros(c_scr.shape, F32)
        n_scr[...] = jnp.zeros(n_scr.shape, F32)
        m_scr[...] = jnp.zeros(m_scr.shape, F32)
        tail_scr[...] = jnp.zeros(tail_scr.shape, F32)

    x = xml_ref[...].reshape(L, W)
    nw = (K - 1) * CS
    last = x[L - nw:, :]
    prev = tail_scr[...]
    crow = lax.broadcasted_iota(jnp.int32, (nw, W), 0) % CS
    wrap = jnp.where(crow == 0, pltpu.roll(prev, nw - (CS - 1), 0), pltpu.roll(last, 1, 0))
    tail_scr[...] = last
    conv = cb_ref[...] + cw_ref[K - 1:K, :] * x
    for d in range(1, K):
        xd = jnp.concatenate([wrap[nw - d * CS:, :], x[:L - d * CS, :]], axis=0)
        conv = conv + cw_ref[K - 1 - d:K - d, :] * xd
    xc = conv * jax.nn.sigmoid(conv)
    xcb = xc.astype(BF16)

    g = if_ref[...].reshape(L, if_ref.shape[2])
    lf = jnp.minimum(g, 0.0) - jnp.log1p(jnp.exp(-jnp.abs(g)))
    ri = lax.broadcasted_iota(jnp.int32, (L, L), 0)
    ci = lax.broadcasted_iota(jnp.int32, (L, L), 1)
    causal = ((ci % CS) * T + ci // CS) <= ((ri % CS) * T + ri // CS)
    bcum = jnp.dot(causal.astype(F32), lf, preferred_element_type=F32, precision=HIGHEST)
    bcum_t = bcum.T
    g_t = g.T

    for h in range(H):
        cs = slice(h * dh, (h + 1) * dh)
        q = (_dot(xcb[:, cs], wq_ref[h].astype(BF16)) * (dh ** -0.5)).astype(BF16)
        kf = _dot(xcb[:, cs], wk_ref[h].astype(BF16))
        vb = v_ref[:, :, cs].reshape(L, dh).astype(BF16)
        b_col = bcum[:, H + h:H + h + 1]
        i_col = g[:, h:h + 1]
        b_row = bcum_t[H + h:H + h + 1, :]
        i_row = g_t[h:h + 1, :]
        m_prev = m_scr[h, 0:1, 0:1]
        c_prev = c_scr[h]
        n_prev = n_scr[h, 0:1, :]

        dmat = jnp.where(causal, b_col - b_row + i_row, -jnp.inf)
        inter = b_col + m_prev
        m_t = jnp.maximum(inter, jnp.max(dmat, axis=-1, keepdims=True))
        w_inter = jnp.exp(inter - m_t)
        s = _dot_nt(q, kf.astype(BF16)) * jnp.exp(dmat - m_t)
        num = w_inter * _dot(q, c_prev.astype(BF16)) + _dot(s.astype(BF16), vb)
        qn = jnp.sum(q.astype(F32) * n_prev, axis=-1, keepdims=True)
        den = w_inter * qn + jnp.sum(s, axis=-1, keepdims=True)
        ht = num / jnp.maximum(jnp.abs(den), jnp.exp(-m_t))

        g_tot = b_row[:, L - 1:L]
        dec_row = g_tot - b_row + i_row
        m_new = jnp.maximum(g_tot + m_prev, jnp.max(dec_row, axis=-1, keepdims=True))
        w_old = jnp.exp(g_tot + m_prev - m_new)
        kw = kf * jnp.exp(g_tot - b_col + i_col - m_new)
        c_scr[h] = w_old * c_prev + _dot_tn(kw.astype(BF16), vb)
        n_scr[h] = jnp.broadcast_to(w_old * n_prev + jnp.sum(kw, axis=0, keepdims=True), n_scr.shape[1:])
        m_scr[h] = jnp.broadcast_to(m_new, m_scr.shape[1:])

        hc = jax.nn.sigmoid(o_ref[:, :, cs].reshape(L, dh)) * ht
        hn = hc * lax.rsqrt(jnp.mean(hc * hc, axis=-1, keepdims=True) + NORM_EPS)
        y_ref[:, :, cs] = (hn * ng_ref[:, cs] + sk_ref[:, cs] * xc[:, cs]).reshape(T, CS, dh)


def _mlstm(xml, v, o, ifg, conv_w, conv_b, wq, wk, norm_g, skip, layer, batch):
    T, B, ncb, W = xml.shape
    H, dh = wq.shape[1], wq.shape[2]
    K = conv_w.shape[1]
    CS = ML_CHUNK_S5

    def seq(w):
        return pl.BlockSpec((T, None, CS, w), lambda b, c: (0, b, c, 0))

    return pl.pallas_call(
        _mlstm_kernel,
        grid=(B, ncb // CS),
        in_specs=[seq(W), seq(W), seq(W), seq(GATE_LANES),
                  _const_spec((K, W), layer), _const_spec((1, W), layer),
                  _const_spec((H, dh, dh), layer), _const_spec((H, dh, dh), layer),
                  _const_spec((1, W), layer), _const_spec((1, W), layer)],
        out_specs=seq(W),
        out_shape=jax.ShapeDtypeStruct((T, B, ncb, W), F32),
        scratch_shapes=[pltpu.VMEM((H, dh, dh), F32), pltpu.VMEM((H, 8, dh), F32),
                        pltpu.VMEM((H, 8, 128), F32), pltpu.VMEM(((K - 1) * CS, W), F32)],
        compiler_params=_params(("parallel", "arbitrary")),
        name="mlstm",
    )(xml, v, o, ifg, conv_w, conv_b, wq, wk, norm_g, skip)


def _gelu_tanh(x):
    return 0.5 * x * (1.0 + jnp.tanh(math.sqrt(2.0 / math.pi) * (x + 0.044715 * (x * x * x))))


def _merge_kernel(x_ref, yt_ref, ut_ref, yml_ref, g_ref, wgate_ref, d_ref, gv_ref, gg_ref,
                  wbs_ref, wbm_ref, wout_ref, o_ref):
    d = x_ref.shape[1]
    part = x_ref.shape[0] // MERGE_SPLIT
    for k in range(MERGE_SPLIT):
        tk = slice(k * part, (k + 1) * part)
        x = x_ref[tk, :]
        h = _rms(x, g_ref[...]).astype(BF16)
        z = _gelu_tanh(yt_ref[:, tk] + d_ref[...] * ut_ref[:, tk]).T.astype(BF16)
        ys5 = _dot(z, gv_ref[...].astype(BF16)) * jax.nn.sigmoid(_dot(z, gg_ref[...].astype(BF16)))
        gate_s5 = jax.nn.sigmoid(_dot_nt(h, wgate_ref[:d, :]))
        mix = gate_s5 * _dot(ys5.astype(BF16), wbs_ref[...].astype(BF16))
        gate_ml = jax.nn.sigmoid(_dot_nt(h, wgate_ref[d:, :]))
        mix = mix + gate_ml * _dot(yml_ref[tk, :].astype(BF16), wbm_ref[...].astype(BF16))
        o_ref[tk, :] = x + _dot(mix.astype(BF16), wout_ref[...].astype(BF16))


def _merge(x, yt, ut, yml, norm, w_gate, d_skip, gv, gg, wbs, wbm, wout, layer):
    n, d = x.shape
    T, s5w, nc = yt.shape
    mlw = yml.shape[1]

    def tok(w):
        return pl.BlockSpec((nc, w), lambda i: (i, 0))

    slab = pl.BlockSpec((None, s5w, nc), lambda i: (i, 0, 0))
    return pl.pallas_call(
        _merge_kernel,
        grid=(T,),
        in_specs=[tok(d), slab, slab, tok(mlw),
                  _const_spec((1, d), layer), _const_spec((2 * d, d), layer),
                  _const_spec((s5w, 1), layer), _const_spec((s5w, s5w), layer),
                  _const_spec((s5w, s5w), layer), _const_spec((s5w, d), layer),
                  _const_spec((mlw, d), layer), _const_spec((d, d), layer)],
        out_specs=tok(d),
        out_shape=jax.ShapeDtypeStruct((n, d), F32),
        compiler_params=_params(("parallel",)),
        name="merge",
    )(x, yt, ut, yml, norm, w_gate, d_skip, gv, gg, wbs, wbm, wout)


def _gate_weight_kernel(w_ref, o_ref, *, start):
    o_ref[...] = w_ref[start:start + o_ref.shape[0], :].astype(BF16)


def _gate_weights(w_in, start, width):
    depth, cols, d = w_in.shape
    return pl.pallas_call(
        functools.partial(_gate_weight_kernel, start=start),
        grid=(depth,),
        in_specs=[pl.BlockSpec((None, cols, d), lambda l: (l, 0, 0), pipeline_mode=pl.Buffered(1))],
        out_specs=pl.BlockSpec((None, width, d), lambda l: (l, 0, 0)),
        out_shape=jax.ShapeDtypeStruct((depth, width, d), BF16),
        compiler_params=_params(("parallel",)),
        name="gate_weights",
    )(w_in)


def kernel(x, ffn1_norm, ffn1_wg, ffn1_wu, ffn1_wd, mix_norm, w_in, b_if, s5_lam_re, s5_lam_im, s5_log_dt, s5_b_re, s5_b_im, s5_c_re, s5_c_im, s5_d, s5_glu_v, s5_glu_g, ml_conv_w, ml_conv_b, ml_wq, ml_wk, ml_norm, ml_skip, w_br_s5, w_br_ml, w_out, ffn2_norm, ffn2_wg, ffn2_wu, ffn2_wd, final_norm):
    B, L, D = x.shape
    depth = w_in.shape[0]
    s5w = s5_d.shape[1]
    mlw = ml_norm.shape[1]
    H = ml_wq.shape[1]
    T = S5_CHUNK
    ncb = L // T
    nc = B * ncb
    assert ncb % S5_SEGMENTS == 0 and ncb % ML_CHUNK_S5 == 0 and (T * nc) % TOKEN_TILE == 0
    o3 = s5w + 3 * mlw
    o4 = o3 + 2 * H

    row = lambda a: a[:, None, :]
    ffn1 = (row(ffn1_norm), ffn1_wg, ffn1_wu, ffn1_wd)
    ffn2 = (row(ffn2_norm), ffn2_wg, ffn2_wu, ffn2_wd)
    b_ifp = row(jnp.pad(b_if, ((0, 0), (0, GATE_LANES - 2 * H))))
    w_in = jnp.swapaxes(w_in, 1, 2)
    w_gate = _gate_weights(w_in, o4, 2 * D)
    mixn, conv_b, ml_n, ml_s = row(mix_norm), row(ml_conv_b), row(ml_norm), row(ml_skip)
    d_skip = s5_d[:, :, None]
    fin = final_norm[None, :]
    s5_params = _s5_pack(s5_lam_re, s5_lam_im, s5_log_dt, s5_b_re, s5_b_im, s5_c_re, s5_c_im)

    xt = x.reshape(B, ncb, T, D).transpose(2, 0, 1, 3).reshape(T * nc, D)
    seq4 = lambda a: a.reshape(T, B, ncb, a.shape[-1])
    for l in range(depth):
        xt = _ffn(xt, *ffn1, fin, l, False)
        ut, xml, v, o, ifg = _inproj(xt, mixn, w_in, b_ifp, l, nc, s5w, mlw, 2 * H)
        yt = _s5_scan(ut, *s5_params, l, B)
        yml = _mlstm(seq4(xml), seq4(v), seq4(o), seq4(ifg), ml_conv_w, conv_b, ml_wq, ml_wk, ml_n, ml_s, l, B)
        xt = _merge(xt, yt, ut, yml.reshape(T * nc, mlw), mixn, w_gate, d_skip, s5_glu_v, s5_glu_g,
                    w_br_s5, w_br_ml, w_out, l)
        xt = _ffn(xt, *ffn2, fin, l, l == depth - 1)
    return xt.reshape(T, B, ncb, D).transpose(1, 2, 0, 3).reshape(B, L, D)
```

```python
import functools
import math

import jax
import jax.numpy as jnp
from jax import lax
from jax.experimental import pallas as pl
from jax.experimental.pallas import tpu as pltpu

F32 = jnp.float32
BF16 = jnp.bfloat16

NORM_EPS = 1e-6
S5_GROUP = 16
S5_CHUNK = 32
S5_SEGMENTS = 8
ML_HEADS = 4
ML_CHUNK_S5 = 8
GATE_LANES = 128
TOKEN_TILE = 512
FF_CHUNK = 256
MERGE_SPLIT = 2
VMEM_LIMIT = 56 * 1024 * 1024
HIGHEST = lax.Precision.HIGHEST


def _rms(x, g):
    return x * lax.rsqrt(jnp.mean(x * x, axis=-1, keepdims=True) + NORM_EPS) * g


def _dot(a, b):
    return jnp.dot(a, b, preferred_element_type=F32)


def _dot_nt(a, b):
    return lax.dot_general(a, b, (((1,), (1,)), ((), ())), preferred_element_type=F32)


def _dot_tn(a, b):
    return lax.dot_general(a, b, (((0,), (0,)), ((), ())), preferred_element_type=F32)


def _const_spec(shape, layer=None):
    nd = len(shape)
    if layer is None:
        return pl.BlockSpec(shape, lambda *_: (0,) * nd, pipeline_mode=pl.Buffered(1))
    return pl.BlockSpec((None,) + tuple(shape), lambda *_: (layer,) + (0,) * nd,
                        pipeline_mode=pl.Buffered(1))


def _params(sem):
    return pltpu.CompilerParams(dimension_semantics=sem, vmem_limit_bytes=VMEM_LIMIT)


def _ffn_kernel(x_ref, g_ref, wg_ref, wu_ref, wd_ref, gf_ref, o_ref, *, final):
    x = x_ref[...]
    h = _rms(x, g_ref[...]).astype(BF16)
    d_ff = wg_ref.shape[1]
    acc = jnp.zeros(x.shape, F32)
    for c in range(d_ff // FF_CHUNK):
        sl = slice(c * FF_CHUNK, (c + 1) * FF_CHUNK)
        g = _dot(h, wg_ref[:, sl].astype(BF16))
        u = _dot(h, wu_ref[:, sl].astype(BF16))
        a = (g * jax.nn.sigmoid(g) * u).astype(BF16)
        acc = acc + _dot(a, wd_ref[sl, :].astype(BF16))
    y = x + 0.5 * acc
    if final:
        y = _rms(y, gf_ref[...])
    o_ref[...] = y


def _ffn(x, norm, wg, wu, wd, final_norm, layer, final):
    n, d = x.shape
    d_ff = wg.shape[2]
    tok = pl.BlockSpec((TOKEN_TILE, d), lambda i: (i, 0))
    return pl.pallas_call(
        functools.partial(_ffn_kernel, final=final),
        grid=(n // TOKEN_TILE,),
        in_specs=[tok, _const_spec((1, d), layer), _const_spec((d, d_ff), layer),
                  _const_spec((d, d_ff), layer), _const_spec((d_ff, d), layer),
                  _const_spec((1, d))],
        out_specs=tok,
        out_shape=jax.ShapeDtypeStruct((n, d), F32),
        compiler_params=_params(("parallel",)),
        name="ffn",
    )(x, norm, wg, wu, wd, final_norm)


def _inproj_kernel(x_ref, g_ref, w_ref, bif_ref, ut_ref, xml_ref, v_ref, o_ref, if_ref, *, n_gates):
    h = _rms(x_ref[...], g_ref[...]).astype(BF16)
    c0 = ut_ref.shape[0]
    c1 = c0 + xml_ref.shape[1]
    c2 = c1 + v_ref.shape[1]
    c3 = c2 + o_ref.shape[1]
    gl = if_ref.shape[1]
    ut_ref[...] = _dot_nt(w_ref[:c0, :].astype(BF16), h)
    xml_ref[...] = _dot_nt(h, w_ref[c0:c1, :].astype(BF16))
    v_ref[...] = _dot_nt(h, w_ref[c1:c2, :].astype(BF16))
    o_ref[...] = _dot_nt(h, w_ref[c2:c3, :].astype(BF16))
    gates = _dot_nt(h, w_ref[c3:c3 + gl, :].astype(BF16))
    lane = lax.broadcasted_iota(jnp.int32, gates.shape, 1)
    if_ref[...] = jnp.where(lane < n_gates, gates, 0.0) + bif_ref[...]


def _inproj(x, norm, w_in, b_if, layer, nc, s5w, mlw, n_gates):
    n, d = x.shape
    used = s5w + 3 * mlw + GATE_LANES
    assert used <= w_in.shape[1] and used % 8 == 0

    def tok(w):
        return pl.BlockSpec((nc, w), lambda i: (i, 0))

    widths = (mlw, mlw, mlw, GATE_LANES)
    return pl.pallas_call(
        functools.partial(_inproj_kernel, n_gates=n_gates),
        grid=(n // nc,),
        in_specs=[tok(d), _const_spec((1, d), layer), _const_spec((used, d), layer),
                  _const_spec((1, GATE_LANES), layer)],
        out_specs=[pl.BlockSpec((None, s5w, nc), lambda i: (i, 0, 0))] + [tok(w) for w in widths],
        out_shape=[jax.ShapeDtypeStruct((n // nc, s5w, nc), F32)]
        + [jax.ShapeDtypeStruct((n, w), F32) for w in widths],
        compiler_params=_params(("parallel",)),
        name="inproj",
    )(x, norm, w_in, b_if)


def _s5_pack(lam_re, lam_im, log_dt, b_re, b_im, c_re, c_im):
    two = lambda a: jnp.concatenate([a, a], axis=-1)
    lam = jnp.stack([two(lam_re), two(lam_im), jnp.broadcast_to(log_dt[..., None], two(lam_re).shape)]
                    + [jnp.zeros_like(two(lam_re))] * 5, axis=-2)
    bmat = jnp.concatenate([b_re, b_im], axis=-2)
    cc = jnp.concatenate([two(c_re), two(c_im)], axis=-2)
    return lam, bmat, cc


def _s5_operators(lam, b1, cc, T, P, R):
    n2 = lam.shape[1]
    N = n2 // 2
    tp = T * P
    lr = jnp.minimum(lam[0:1, :], -1e-4)
    li = lam[1:2, :]
    dt = jnp.exp(lam[2:3, :])
    lrdt = lr * dt
    lidt = li * dt
    half = lax.broadcasted_iota(jnp.int32, (1, n2), 1) < N

    def powers(k):
        pm = jnp.exp(lrdt * k)
        return pm * jnp.cos(lidt * k), pm * jnp.sin(lidt * k)

    kk = lax.broadcasted_iota(jnp.int32, (T + 8, 1), 0).astype(F32)
    pre, pim = powers(kk)
    rr = lax.broadcasted_iota(jnp.int32, (R + 8, 1), 0).astype(F32) * float(T)
    sre, sim = powers(rr)

    def forms(re, im):
        f2 = jnp.where(half, -im, im)
        return re, f2, -f2

    def to_columns(rows):
        pad = jnp.zeros((n2 - rows.shape[0], n2), F32)
        return jnp.concatenate([rows, pad], axis=0).T

    ab_re, ab_im = pre[1:2, :], pim[1:2, :]
    den = lr * lr + li * li
    q_re = ((ab_re - 1.0) * lr + ab_im * li) / den
    q_im = (ab_im * lr - (ab_re - 1.0) * li) / den
    q_col = to_columns(jnp.concatenate([q_re, q_im], axis=0))
    q_re_c, q_im_c = q_col[:, 0:1], q_col[:, 1:2]
    b2 = jnp.concatenate([-b1[N:, :], b1[:N, :]], axis=0)
    bbs = q_re_c * b1 + q_im_c * b2
    bbx = q_re_c * b2 - q_im_c * b1

    cc1, cc2 = cc[0:P, :], cc[P:2 * P, :]
    pp1 = jnp.where(half, pre, -pim)
    pp2 = jnp.where(half, -pim, -pre)

    def readout(k0):
        return jnp.concatenate([pp1[k:k + 1, :] * cc1 + pp2[k:k + 1, :] * cc2 for k in range(k0, k0 + T)], axis=0)

    v = readout(1).astype(BF16)
    kcol = jnp.dot(readout(0), bbs, preferred_element_type=F32, precision=HIGHEST).astype(BF16)

    col_i = lax.broadcasted_iota(jnp.int32, (n2, tp), 1) // P
    pick_i = jnp.where(col_i == T - 1 - lax.broadcasted_iota(jnp.int32, (n2, tp), 0), 1.0, 0.0).astype(BF16)
    col_p = lax.broadcasted_iota(jnp.int32, (P, tp), 1) % P
    pick_p = jnp.where(col_p == lax.broadcasted_iota(jnp.int32, (P, tp), 0), 1.0, 0.0).astype(BF16)

    def expand(a, e):
        hi = a.astype(BF16)
        return _dot(hi, e) + _dot((a - hi.astype(F32)).astype(BF16), e)

    bt1 = expand(bbs, pick_p)
    bt2 = jnp.concatenate([-bt1[N:, :], bt1[:N, :]], axis=0)
    w_s = expand(to_columns(pre[0:T, :]), pick_i) * bt1 + expand(to_columns(pim[0:T, :]), pick_i) * bt2
    w = jnp.concatenate([w_s, w_s[N:, :], w_s[:N, :]], axis=0).astype(BF16)

    step = forms(pre[T:T + 1, :], pim[T:T + 1, :])
    seg = forms(sre[R:R + 1, :], sim[R:R + 1, :])
    pw1, pw2, _ = forms(sre[0:R, :], sim[0:R, :])
    return kcol, w, v, step, seg, pw1, pw2


def _toeplitz(kcol, T, P):
    tp = T * P
    lane = lax.broadcasted_iota(jnp.int32, (P, tp), 1)
    sub = lax.broadcasted_iota(jnp.int32, (P, tp), 0)
    rep = jnp.where(lane % P == sub, 1.0, 0.0).astype(BF16)
    m = _dot(kcol, rep)
    blk = lax.broadcasted_iota(jnp.int32, (tp, tp), 1) // P
    shift = P
    while shift < tp:
        moved = jnp.concatenate([jnp.zeros((shift, tp), F32), m[:tp - shift, :]], axis=0)
        m = jnp.where((blk & (shift // P)) != 0, moved, m)
        shift *= 2
    return m.astype(BF16)


def _s5_kernel(u_ref, lam_ref, b_ref, cc_ref, y_ref, *, batch):
    T, P, nct = u_ref.shape
    n2 = lam_ref.shape[1]
    S = S5_SEGMENTS
    ncb = nct // batch
    R = ncb // S
    kcol, w, v, (c1, c2, c2t), (g1, g2, g2t), pw1, pw2 = _s5_operators(
        lam_ref[...], b_ref[...], cc_ref[...], T, P, R)
    u = u_ref[...].reshape(T * P, nct).astype(BF16)
    m = _toeplitz(kcol, T, P)
    x = _dot(w, u).T
    xr = [jnp.swapaxes(x[b * ncb:(b + 1) * ncb, :].reshape(S, R, 2 * n2), 0, 1).reshape(ncb, 2 * n2)
          for b in range(batch)]
    rows = lax.broadcasted_iota(jnp.int32, (S, n2), 0)

    s = [jnp.zeros((S, n2), F32)] * batch
    t = [jnp.zeros((S, n2), F32)] * batch
    local = [[] for _ in range(batch)]
    for r in range(R):
        for b in range(batch):
            local[b].append(s[b])
            xin = xr[b][r * S:(r + 1) * S, :]
            s[b], t[b] = c1 * s[b] + c2 * t[b] + xin[:, :n2], c1 * t[b] + c2t * s[b] + xin[:, n2:]
    ini_s = [jnp.zeros((S, n2), F32)] * batch
    ini_t = [jnp.zeros((S, n2), F32)] * batch
    cur_s = [jnp.zeros((1, n2), F32)] * batch
    cur_t = [jnp.zeros((1, n2), F32)] * batch
    for k in range(1, S):
        for b in range(batch):
            cur_s[b], cur_t[b] = (g1 * cur_s[b] + g2 * cur_t[b] + s[b][k - 1:k, :],
                                  g1 * cur_t[b] + g2t * cur_s[b] + t[b][k - 1:k, :])
            ini_s[b] = jnp.where(rows == k, cur_s[b], ini_s[b])
            ini_t[b] = jnp.where(rows == k, cur_t[b], ini_t[b])
    before = []
    for b in range(batch):
        sb = jnp.concatenate([local[b][r] + pw1[r:r + 1, :] * ini_s[b] + pw2[r:r + 1, :] * ini_t[b]
                              for r in range(R)], axis=0)
        before.append(jnp.swapaxes(sb.reshape(R, S, n2), 0, 1).reshape(ncb, n2))
    y = _dot(m, u) + _dot_nt(v, jnp.concatenate(before, axis=0).astype(BF16))
    y_ref[...] = y.reshape(T, P, nct)


def _s5_scan(ut, lam, bmat, cc, layer, batch):
    T, s5w, nct = ut.shape
    G = s5w // S5_GROUP
    n2 = lam.shape[-1]

    def grp(a):
        return pl.BlockSpec((None, None) + a.shape[2:], lambda g: (layer, g, 0, 0))

    slab = pl.BlockSpec((T, S5_GROUP, nct), lambda g: (0, g, 0))
    return pl.pallas_call(
        functools.partial(_s5_kernel, batch=batch),
        grid=(G,),
        in_specs=[slab, grp(lam), grp(bmat), grp(cc)],
        out_specs=slab,
        out_shape=jax.ShapeDtypeStruct((T, s5w, nct), F32),
        compiler_params=_params(("parallel",)),
        name="s5_scan",
    )(ut, lam, bmat, cc)


def _mlstm_kernel(xml_ref, v_ref, o_ref, if_ref, cw_ref, cb_ref, wq_ref, wk_ref, ng_ref, sk_ref,
                  y_ref, c_scr, n_scr, m_scr, tail_scr, tri_scr, neg_scr):
    H = wq_ref.shape[0]
    dh = wq_ref.shape[1]
    T, B, CS, W = xml_ref.shape
    L = T * CS
    K = cw_ref.shape[0]

    @pl.when(pl.program_id(0) == 0)
    def _():
        c_scr[...] = jnp.zeros(c_scr.shape, F32)
        n_scr[...] = jnp.zeros(n_scr.shape, F32)
        m_scr[...] = jnp.zeros(m_scr.shape, F32)
        tail_scr[...] = jnp.zeros(tail_scr.shape, F32)
        ri = lax.broadcasted_iota(jnp.int32, (L, L), 0)
        ci = lax.broadcasted_iota(jnp.int32, (L, L), 1)
        causal = ((ci % CS) * T + ci // CS) <= ((ri % CS) * T + ri // CS)
        tri_scr[...] = jnp.where(causal, 1.0, 0.0)
        neg_scr[...] = jnp.where(causal, 0.0, -jnp.inf)

    nw = (K - 1) * CS
    crow = lax.broadcasted_iota(jnp.int32, (nw, W), 0) % CS
    for b, h in [(b, h) for b in range(B) for h in range(H)]:
        if h == 0:
            x = xml_ref[:, b, :, :].reshape(L, W)
            last = x[L - nw:, :]
            wrap = jnp.where(crow == 0, pltpu.roll(tail_scr[b], nw - (CS - 1), 0), pltpu.roll(last, 1, 0))
            tail_scr[b] = last
            conv = cb_ref[...] + cw_ref[K - 1:K, :] * x
            for d in range(1, K):
                xd = jnp.concatenate([wrap[nw - d * CS:, :], x[:L - d * CS, :]], axis=0)
                conv = conv + cw_ref[K - 1 - d:K - d, :] * xd
            xc = conv * jax.nn.sigmoid(conv)
            xcb = xc.astype(BF16)
            g = if_ref[:, b, :, :].reshape(L, if_ref.shape[3])
            lf = jnp.minimum(g, 0.0) - jnp.log1p(jnp.exp(-jnp.abs(g)))
            bcum = jnp.dot(tri_scr[...], lf, preferred_element_type=F32, precision=HIGHEST)
            bcum_t = bcum.T
            g_t = g.T
        st = b * H + h
        cs = slice(h * dh, (h + 1) * dh)
        q = (_dot(xcb[:, cs], wq_ref[h].astype(BF16)) * (dh ** -0.5)).astype(BF16)
        kf = _dot(xcb[:, cs], wk_ref[h].astype(BF16))
        vb = v_ref[:, b, :, cs].reshape(L, dh).astype(BF16)
        b_col = bcum[:, H + h:H + h + 1]
        i_col = g[:, h:h + 1]
        b_row = bcum_t[H + h:H + h + 1, :]
        i_row = g_t[h:h + 1, :]
        m_prev = m_scr[st, 0:1, 0:1]
        c_prev = c_scr[st]
        n_prev = n_scr[st, 0:1, :]

        dmat = b_col + (i_row - b_row) + neg_scr[...]
        inter = b_col + m_prev
        m_t = jnp.maximum(inter, jnp.max(dmat, axis=-1, keepdims=True))
        w_inter = jnp.exp(inter - m_t)
        s = _dot_nt(q, kf.astype(BF16)) * jnp.exp(dmat - m_t)
        num = w_inter * _dot(q, c_prev.astype(BF16)) + _dot(s.astype(BF16), vb)
        qn = jnp.sum(q.astype(F32) * n_prev, axis=-1, keepdims=True)
        den = w_inter * qn + jnp.sum(s, axis=-1, keepdims=True)
        ht = num / jnp.maximum(jnp.abs(den), jnp.exp(-m_t))

        g_tot = b_row[:, L - 1:L]
        dec_row = g_tot - b_row + i_row
        m_new = jnp.maximum(g_tot + m_prev, jnp.max(dec_row, axis=-1, keepdims=True))
        w_old = jnp.exp(g_tot + m_prev - m_new)
        kw = kf * jnp.exp(g_tot - b_col + i_col - m_new)
        c_scr[st] = w_old * c_prev + _dot_tn(kw.astype(BF16), vb)
        n_scr[st] = jnp.broadcast_to(w_old * n_prev + jnp.sum(kw, axis=0, keepdims=True), n_scr.shape[1:])
        m_scr[st] = jnp.broadcast_to(m_new, m_scr.shape[1:])

        hc = jax.nn.sigmoid(o_ref[:, b, :, cs].reshape(L, dh)) * ht
        hn = hc * lax.rsqrt(jnp.mean(hc * hc, axis=-1, keepdims=True) + NORM_EPS)
        y_ref[:, b, :, cs] = (hn * ng_ref[:, cs] + sk_ref[:, cs] * xc[:, cs]).reshape(T, CS, dh)


def _mlstm(xml, v, o, ifg, conv_w, conv_b, wq, wk, norm_g, skip, layer):
    T, B, ncb, W = xml.shape
    H, dh = wq.shape[1], wq.shape[2]
    K = conv_w.shape[1]
    CS = ML_CHUNK_S5
    L = T * CS

    def seq(w):
        return pl.BlockSpec((T, B, CS, w), lambda c: (0, 0, c, 0))

    return pl.pallas_call(
        _mlstm_kernel,
        grid=(ncb // CS,),
        in_specs=[seq(W), seq(W), seq(W), seq(GATE_LANES),
                  _const_spec((K, W), layer), _const_spec((1, W), layer),
                  _const_spec((H, dh, dh), layer), _const_spec((H, dh, dh), layer),
                  _const_spec((1, W), layer), _const_spec((1, W), layer)],
        out_specs=seq(W),
        out_shape=jax.ShapeDtypeStruct((T, B, ncb, W), F32),
        scratch_shapes=[pltpu.VMEM((B * H, dh, dh), F32), pltpu.VMEM((B * H, 8, dh), F32),
                        pltpu.VMEM((B * H, 8, 128), F32), pltpu.VMEM((B, (K - 1) * CS, W), F32),
                        pltpu.VMEM((L, L), F32), pltpu.VMEM((L, L), F32)],
        compiler_params=_params(("arbitrary",)),
        name="mlstm",
    )(xml, v, o, ifg, conv_w, conv_b, wq, wk, norm_g, skip)


def _gelu_tanh(x):
    return 0.5 * x * (1.0 + jnp.tanh(math.sqrt(2.0 / math.pi) * (x + 0.044715 * (x * x * x))))


def _merge_kernel(x_ref, yt_ref, ut_ref, yml_ref, g_ref, wgate_ref, d_ref, gv_ref, gg_ref,
                  wbs_ref, wbm_ref, wout_ref, o_ref):
    d = x_ref.shape[1]
    part = x_ref.shape[0] // MERGE_SPLIT
    for k in range(MERGE_SPLIT):
        tk = slice(k * part, (k + 1) * part)
        x = x_ref[tk, :]
        h = _rms(x, g_ref[...]).astype(BF16)
        z = _gelu_tanh(yt_ref[:, tk] + d_ref[...] * ut_ref[:, tk]).T.astype(BF16)
        ys5 = _dot(z, gv_ref[...].astype(BF16)) * jax.nn.sigmoid(_dot(z, gg_ref[...].astype(BF16)))
        gate_s5 = jax.nn.sigmoid(_dot_nt(h, wgate_ref[0, :d, :].astype(BF16)))
        mix = gate_s5 * _dot(ys5.astype(BF16), wbs_ref[...].astype(BF16))
        gate_ml = jax.nn.sigmoid(_dot_nt(h, wgate_ref[0, d:, :].astype(BF16)))
        mix = mix + gate_ml * _dot(yml_ref[tk, :].astype(BF16), wbm_ref[...].astype(BF16))
        o_ref[tk, :] = x + _dot(mix.astype(BF16), wout_ref[...].astype(BF16))


def _merge(x, yt, ut, yml, norm, w_in, gate_row, d_skip, gv, gg, wbs, wbm, wout, layer):
    n, d = x.shape
    T, s5w, nc = yt.shape
    mlw = yml.shape[1]

    def tok(w):
        return pl.BlockSpec((nc, w), lambda i: (i, 0))

    slab = pl.BlockSpec((None, s5w, nc), lambda i: (i, 0, 0))
    gate_rows = pl.BlockSpec((pl.Element(1), pl.Element(2 * d), pl.Element(d)),
                             lambda i: (layer, gate_row, 0), pipeline_mode=pl.Buffered(1))
    return pl.pallas_call(
        _merge_kernel,
        grid=(T,),
        in_specs=[tok(d), slab, slab, tok(mlw),
                  _const_spec((1, d), layer), gate_rows,
                  _const_spec((s5w, 1), layer), _const_spec((s5w, s5w), layer),
                  _const_spec((s5w, s5w), layer), _const_spec((s5w, d), layer),
                  _const_spec((mlw, d), layer), _const_spec((d, d), layer)],
        out_specs=tok(d),
        out_shape=jax.ShapeDtypeStruct((n, d), F32),
        compiler_params=_params(("parallel",)),
        name="merge",
    )(x, yt, ut, yml, norm, w_in, d_skip, gv, gg, wbs, wbm, wout)


def kernel(x, ffn1_norm, ffn1_wg, ffn1_wu, ffn1_wd, mix_norm, w_in, b_if, s5_lam_re, s5_lam_im, s5_log_dt, s5_b_re, s5_b_im, s5_c_re, s5_c_im, s5_d, s5_glu_v, s5_glu_g, ml_conv_w, ml_conv_b, ml_wq, ml_wk, ml_norm, ml_skip, w_br_s5, w_br_ml, w_out, ffn2_norm, ffn2_wg, ffn2_wu, ffn2_wd, final_norm):
    B, L, D = x.shape
    depth = w_in.shape[0]
    s5w = s5_d.shape[1]
    mlw = ml_norm.shape[1]
    H = ml_wq.shape[1]
    T = S5_CHUNK
    ncb = L // T
    nc = B * ncb
    assert ncb % S5_SEGMENTS == 0 and ncb % ML_CHUNK_S5 == 0 and (T * nc) % TOKEN_TILE == 0
    o3 = s5w + 3 * mlw
    o4 = o3 + 2 * H

    row = lambda a: a[:, None, :]
    ffn1 = (row(ffn1_norm), ffn1_wg, ffn1_wu, ffn1_wd)
    ffn2 = (row(ffn2_norm), ffn2_wg, ffn2_wu, ffn2_wd)
    b_ifp = row(jnp.pad(b_if, ((0, 0), (0, GATE_LANES - 2 * H))))
    w_in = jnp.swapaxes(w_in, 1, 2)
    mixn, conv_b, ml_n, ml_s = row(mix_norm), row(ml_conv_b), row(ml_norm), row(ml_skip)
    d_skip = s5_d[:, :, None]
    fin = final_norm[None, :]
    s5_params = _s5_pack(s5_lam_re, s5_lam_im, s5_log_dt, s5_b_re, s5_b_im, s5_c_re, s5_c_im)

    xt = x.reshape(B, ncb, T, D).transpose(2, 0, 1, 3).reshape(T * nc, D)
    seq4 = lambda a: a.reshape(T, B, ncb, a.shape[-1])
    for l in range(depth):
        xt = _ffn(xt, *ffn1, fin, l, False)
        ut, xml, v, o, ifg = _inproj(xt, mixn, w_in, b_ifp, l, nc, s5w, mlw, 2 * H)
        yt = _s5_scan(ut, *s5_params, l, B)
        yml = _mlstm(seq4(xml), seq4(v), seq4(o), seq4(ifg), ml_conv_w, conv_b, ml_wq, ml_wk, ml_n, ml_s, l)
        xt = _merge(xt, yt, ut, yml.reshape(T * nc, mlw), mixn, w_in, o4, d_skip, s5_glu_v, s5_glu_g,
                    w_br_s5, w_br_ml, w_out, l)
        xt = _ffn(xt, *ffn2, fin, l, l == depth - 1)
    return xt.reshape(T, B, ncb, D).transpose(1, 2, 0, 3).reshape(B, L, D)
```

```python
import functools
import math

import jax
import jax.numpy as jnp
from jax import lax
from jax.experimental import pallas as pl
from jax.experimental.pallas import tpu as pltpu

F32 = jnp.float32
BF16 = jnp.bfloat16

NORM_EPS = 1e-6
S5_GROUP = 16
S5_CHUNK = 32
S5_SEGMENTS = 8
S5_GROUPS_PER_STEP = 4
ML_HEADS = 4
ML_CHUNK_S5 = 8
GATE_LANES = 128
TOKEN_TILE = 512
FF_CHUNK = 256
MERGE_SPLIT = 2
VMEM_LIMIT = 56 * 1024 * 1024
HIGHEST = lax.Precision.HIGHEST


def _rms(x, g):
    return x * lax.rsqrt(jnp.mean(x * x, axis=-1, keepdims=True) + NORM_EPS) * g


def _dot(a, b):
    return jnp.dot(a, b, preferred_element_type=F32)


def _dot_nt(a, b):
    return lax.dot_general(a, b, (((1,), (1,)), ((), ())), preferred_element_type=F32)


def _dot_tn(a, b):
    return lax.dot_general(a, b, (((0,), (0,)), ((), ())), preferred_element_type=F32)


def _const_spec(shape, layer=None):
    nd = len(shape)
    if layer is None:
        return pl.BlockSpec(shape, lambda *_: (0,) * nd, pipeline_mode=pl.Buffered(1))
    return pl.BlockSpec((None,) + tuple(shape), lambda *_: (layer,) + (0,) * nd,
                        pipeline_mode=pl.Buffered(1))


def _params(sem):
    return pltpu.CompilerParams(dimension_semantics=sem, vmem_limit_bytes=VMEM_LIMIT)


def _ffn_kernel(x_ref, g_ref, wg_hbm, wu_hbm, wd_hbm, gf_ref, o_ref, wg_ref, wu_ref, wd_ref, sem, *,
                layer, final):
    d_ff = wg_ref.shape[1]
    n_chunks = d_ff // FF_CHUNK
    chunk = lambda c: slice(c * FF_CHUNK, (c + 1) * FF_CHUNK)

    def copies(c):
        return (pltpu.make_async_copy(wg_hbm.at[layer, :, chunk(c)], wg_ref.at[:, chunk(c)], sem.at[0, c]),
                pltpu.make_async_copy(wu_hbm.at[layer, :, chunk(c)], wu_ref.at[:, chunk(c)], sem.at[1, c]),
                pltpu.make_async_copy(wd_hbm.at[layer, chunk(c), :], wd_ref.at[chunk(c), :], sem.at[2, c]))

    def compute(first_step):
        x = x_ref[...]
        h = _rms(x, g_ref[...]).astype(BF16)
        acc = jnp.zeros(x.shape, F32)
        for c in range(n_chunks):
            if first_step:
                for cp in copies(c):
                    cp.wait()
            g = _dot(h, wg_ref[:, chunk(c)].astype(BF16))
            u = _dot(h, wu_ref[:, chunk(c)].astype(BF16))
            a = (g * jax.nn.sigmoid(g) * u).astype(BF16)
            acc = acc + _dot(a, wd_ref[chunk(c), :].astype(BF16))
        y = x + 0.5 * acc
        if final:
            y = _rms(y, gf_ref[...])
        o_ref[...] = y

    @pl.when(pl.program_id(0) == 0)
    def _():
        for c in range(n_chunks):
            for cp in copies(c):
                cp.start()
        compute(True)

    @pl.when(pl.program_id(0) != 0)
    def _():
        compute(False)


def _ffn(x, norm, wg, wu, wd, final_norm, layer, final):
    n, d = x.shape
    d_ff = wg.shape[2]
    tok = pl.BlockSpec((TOKEN_TILE, d), lambda i: (i, 0))
    hbm = pl.BlockSpec(memory_space=pl.ANY)
    return pl.pallas_call(
        functools.partial(_ffn_kernel, layer=layer, final=final),
        grid=(n // TOKEN_TILE,),
        in_specs=[tok, _const_spec((1, d), layer), hbm, hbm, hbm, _const_spec((1, d))],
        out_specs=tok,
        out_shape=jax.ShapeDtypeStruct((n, d), F32),
        scratch_shapes=[pltpu.VMEM((d, d_ff), F32), pltpu.VMEM((d, d_ff), F32), pltpu.VMEM((d_ff, d), F32),
                        pltpu.SemaphoreType.DMA((3, d_ff // FF_CHUNK))],
        compiler_params=_params(("arbitrary",)),
        name="ffn",
    )(x, norm, wg, wu, wd, final_norm)


def _inproj_kernel(x_ref, g_ref, w_ref, bif_ref, ut_ref, xml_ref, v_ref, o_ref, if_ref, *, n_gates):
    h = _rms(x_ref[...], g_ref[...]).astype(BF16)
    c0 = ut_ref.shape[0]
    c1 = c0 + xml_ref.shape[1]
    c2 = c1 + v_ref.shape[1]
    c3 = c2 + o_ref.shape[1]
    gl = if_ref.shape[1]
    ut_ref[...] = _dot_nt(w_ref[:c0, :].astype(BF16), h)
    xml_ref[...] = _dot_nt(h, w_ref[c0:c1, :].astype(BF16))
    v_ref[...] = _dot_nt(h, w_ref[c1:c2, :].astype(BF16))
    o_ref[...] = _dot_nt(h, w_ref[c2:c3, :].astype(BF16))
    gates = _dot_nt(h, w_ref[c3:c3 + gl, :].astype(BF16))
    lane = lax.broadcasted_iota(jnp.int32, gates.shape, 1)
    if_ref[...] = jnp.where(lane < n_gates, gates, 0.0) + bif_ref[...]


def _inproj(x, norm, w_in, b_if, layer, nc, s5w, mlw, n_gates):
    n, d = x.shape
    used = s5w + 3 * mlw + GATE_LANES
    assert used <= w_in.shape[1] and used % 8 == 0

    def tok(w):
        return pl.BlockSpec((nc, w), lambda i: (i, 0))

    widths = (mlw, mlw, mlw, GATE_LANES)
    return pl.pallas_call(
        functools.partial(_inproj_kernel, n_gates=n_gates),
        grid=(n // nc,),
        in_specs=[tok(d), _const_spec((1, d), layer), _const_spec((used, d), layer),
                  _const_spec((1, GATE_LANES), layer)],
        out_specs=[pl.BlockSpec((None, s5w, nc), lambda i: (i, 0, 0))] + [tok(w) for w in widths],
        out_shape=[jax.ShapeDtypeStruct((n // nc, s5w, nc), F32)]
        + [jax.ShapeDtypeStruct((n, w), F32) for w in widths],
        compiler_params=_params(("parallel",)),
        name="inproj",
    )(x, norm, w_in, b_if)


def _s5_pack(lam_re, lam_im, log_dt, b_re, b_im, c_re, c_im):
    two = lambda a: jnp.concatenate([a, a], axis=-1)
    lam = jnp.stack([two(lam_re), two(lam_im), jnp.broadcast_to(log_dt[..., None], two(lam_re).shape)]
                    + [jnp.zeros_like(two(lam_re))] * 5, axis=-2)
    bmat = jnp.concatenate([b_re, b_im], axis=-2)
    cc = jnp.concatenate([two(c_re), two(c_im)], axis=-2)
    return lam, bmat, cc


def _s5_operators(lam, b1, cc, T, P, R):
    n2 = lam.shape[1]
    N = n2 // 2
    tp = T * P
    lr = jnp.minimum(lam[0:1, :], -1e-4)
    li = lam[1:2, :]
    dt = jnp.exp(lam[2:3, :])
    lrdt = lr * dt
    lidt = li * dt
    half = lax.broadcasted_iota(jnp.int32, (1, n2), 1) < N

    def powers(k):
        pm = jnp.exp(lrdt * k)
        return pm * jnp.cos(lidt * k), pm * jnp.sin(lidt * k)

    kk = lax.broadcasted_iota(jnp.int32, (T + 8, 1), 0).astype(F32)
    pre, pim = powers(kk)
    rr = lax.broadcasted_iota(jnp.int32, (R + 8, 1), 0).astype(F32) * float(T)
    sre, sim = powers(rr)

    def forms(re, im):
        f2 = jnp.where(half, -im, im)
        return re, f2, -f2

    def to_columns(rows):
        pad = jnp.zeros((n2 - rows.shape[0], n2), F32)
        return jnp.concatenate([rows, pad], axis=0).T

    ab_re, ab_im = pre[1:2, :], pim[1:2, :]
    den = lr * lr + li * li
    q_re = ((ab_re - 1.0) * lr + ab_im * li) / den
    q_im = (ab_im * lr - (ab_re - 1.0) * li) / den
    q_col = to_columns(jnp.concatenate([q_re, q_im], axis=0))
    q_re_c, q_im_c = q_col[:, 0:1], q_col[:, 1:2]
    b2 = jnp.concatenate([-b1[N:, :], b1[:N, :]], axis=0)
    bbs = q_re_c * b1 + q_im_c * b2
    bbx = q_re_c * b2 - q_im_c * b1

    cc1, cc2 = cc[0:P, :], cc[P:2 * P, :]
    pp1 = jnp.where(half, pre, -pim)
    pp2 = jnp.where(half, -pim, -pre)

    def readout(k0):
        return jnp.concatenate([pp1[k:k + 1, :] * cc1 + pp2[k:k + 1, :] * cc2 for k in range(k0, k0 + T)], axis=0)

    v = readout(1).astype(BF16)
    kcol = jnp.dot(readout(0), bbs, preferred_element_type=F32, precision=HIGHEST).astype(BF16)

    col_i = lax.broadcasted_iota(jnp.int32, (n2, tp), 1) // P
    pick_i = jnp.where(col_i == T - 1 - lax.broadcasted_iota(jnp.int32, (n2, tp), 0), 1.0, 0.0).astype(BF16)
    col_p = lax.broadcasted_iota(jnp.int32, (P, tp), 1) % P
    pick_p = jnp.where(col_p == lax.broadcasted_iota(jnp.int32, (P, tp), 0), 1.0, 0.0).astype(BF16)

    def expand(a, e):
        hi = a.astype(BF16)
        return _dot(hi, e) + _dot((a - hi.astype(F32)).astype(BF16), e)

    bt1 = expand(bbs, pick_p)
    bt2 = jnp.concatenate([-bt1[N:, :], bt1[:N, :]], axis=0)
    w_s = expand(to_columns(pre[0:T, :]), pick_i) * bt1 + expand(to_columns(pim[0:T, :]), pick_i) * bt2
    w = jnp.concatenate([w_s, w_s[N:, :], w_s[:N, :]], axis=0).astype(BF16)

    step = forms(pre[T:T + 1, :], pim[T:T + 1, :])
    seg = forms(sre[R:R + 1, :], sim[R:R + 1, :])
    pw1, pw2, _ = forms(sre[0:R, :], sim[0:R, :])
    return kcol, w, v, step, seg, pw1, pw2


def _toeplitz(kcol, T, P):
    tp = T * P
    lane = lax.broadcasted_iota(jnp.int32, (P, tp), 1)
    sub = lax.broadcasted_iota(jnp.int32, (P, tp), 0)
    rep = jnp.where(lane % P == sub, 1.0, 0.0).astype(BF16)
    m = _dot(kcol, rep)
    blk = lax.broadcasted_iota(jnp.int32, (tp, tp), 1) // P
    shift = P
    while shift < tp:
        moved = jnp.concatenate([jnp.zeros((shift, tp), F32), m[:tp - shift, :]], axis=0)
        m = jnp.where((blk & (shift // P)) != 0, moved, m)
        shift *= 2
    return m.astype(BF16)


def _s5_kernel(u_ref, lam_ref, b_ref, cc_ref, y_ref, *, batch):
    P = S5_GROUP
    for j in range(lam_ref.shape[0]):
        rows = slice(j * P, (j + 1) * P)
        _s5_group(u_ref.at[:, rows, :], lam_ref[j], b_ref[j], cc_ref[j], y_ref.at[:, rows, :], batch)


def _s5_group(u_ref, lam, b1, cc, y_ref, batch):
    T, P, nct = u_ref.shape
    n2 = lam.shape[1]
    S = S5_SEGMENTS
    ncb = nct // batch
    R = ncb // S
    kcol, w, v, (c1, c2, c2t), (g1, g2, g2t), pw1, pw2 = _s5_operators(lam, b1, cc, T, P, R)
    u = u_ref[...].reshape(T * P, nct).astype(BF16)
    m = _toeplitz(kcol, T, P)
    x = _dot(w, u).T
    xr = [jnp.swapaxes(x[b * ncb:(b + 1) * ncb, :].reshape(S, R, 2 * n2), 0, 1).reshape(ncb, 2 * n2)
          for b in range(batch)]
    rows = lax.broadcasted_iota(jnp.int32, (S, n2), 0)

    s = [jnp.zeros((S, n2), F32)] * batch
    t = [jnp.zeros((S, n2), F32)] * batch
    local = [[] for _ in range(batch)]
    for r in range(R):
        for b in range(batch):
            local[b].append(s[b])
            xin = xr[b][r * S:(r + 1) * S, :]
            s[b], t[b] = c1 * s[b] + c2 * t[b] + xin[:, :n2], c1 * t[b] + c2t * s[b] + xin[:, n2:]
    ini_s = [jnp.zeros((S, n2), F32)] * batch
    ini_t = [jnp.zeros((S, n2), F32)] * batch
    cur_s = [jnp.zeros((1, n2), F32)] * batch
    cur_t = [jnp.zeros((1, n2), F32)] * batch
    for k in range(1, S):
        for b in range(batch):
            cur_s[b], cur_t[b] = (g1 * cur_s[b] + g2 * cur_t[b] + s[b][k - 1:k, :],
                                  g1 * cur_t[b] + g2t * cur_s[b] + t[b][k - 1:k, :])
            ini_s[b] = jnp.where(rows == k, cur_s[b], ini_s[b])
            ini_t[b] = jnp.where(rows == k, cur_t[b], ini_t[b])
    before = []
    for b in range(batch):
        sb = jnp.concatenate([local[b][r] + pw1[r:r + 1, :] * ini_s[b] + pw2[r:r + 1, :] * ini_t[b]
                              for r in range(R)], axis=0)
        before.append(jnp.swapaxes(sb.reshape(R, S, n2), 0, 1).reshape(ncb, n2))
    y = _dot(m, u) + _dot_nt(v, jnp.concatenate(before, axis=0).astype(BF16))
    y_ref[...] = y.reshape(T, P, nct)


def _s5_scan(ut, lam, bmat, cc, layer, batch):
    T, s5w, nct = ut.shape
    G = s5w // S5_GROUP
    gs = S5_GROUPS_PER_STEP

    def grp(a):
        return pl.BlockSpec((None, gs) + a.shape[2:], lambda g: (layer, g, 0, 0))

    slab = pl.BlockSpec((T, gs * S5_GROUP, nct), lambda g: (0, g, 0))
    return pl.pallas_call(
        functools.partial(_s5_kernel, batch=batch),
        grid=(G // gs,),
        in_specs=[slab, grp(lam), grp(bmat), grp(cc)],
        out_specs=slab,
        out_shape=jax.ShapeDtypeStruct((T, s5w, nct), F32),
        compiler_params=_params(("parallel",)),
        name="s5_scan",
    )(ut, lam, bmat, cc)


def _mlstm_kernel(xml_ref, v_ref, o_ref, if_ref, cw_ref, cb_ref, wq_ref, wk_ref, ng_ref, sk_ref,
                  y_ref, c_scr, n_scr, m_scr, tail_scr, tri_scr, neg_scr):
    H = wq_ref.shape[0]
    dh = wq_ref.shape[1]
    T, B, CS, W = xml_ref.shape
    L = T * CS
    K = cw_ref.shape[0]

    @pl.when(pl.program_id(0) == 0)
    def _():
        c_scr[...] = jnp.zeros(c_scr.shape, F32)
        n_scr[...] = jnp.zeros(n_scr.shape, F32)
        m_scr[...] = jnp.zeros(m_scr.shape, F32)
        tail_scr[...] = jnp.zeros(tail_scr.shape, F32)
        ri = lax.broadcasted_iota(jnp.int32, (L, L), 0)
        ci = lax.broadcasted_iota(jnp.int32, (L, L), 1)
        causal = ((ci % CS) * T + ci // CS) <= ((ri % CS) * T + ri // CS)
        tri_scr[...] = jnp.where(causal, 1.0, 0.0)
        neg_scr[...] = jnp.where(causal, 0.0, -jnp.inf)

    nw = (K - 1) * CS
    crow = lax.broadcasted_iota(jnp.int32, (nw, W), 0) % CS
    for b, h in [(b, h) for b in range(B) for h in range(H)]:
        if h == 0:
            x = xml_ref[:, b, :, :].reshape(L, W)
            last = x[L - nw:, :]
            wrap = jnp.where(crow == 0, pltpu.roll(tail_scr[b], nw - (CS - 1), 0), pltpu.roll(last, 1, 0))
            tail_scr[b] = last
            conv = cb_ref[...] + cw_ref[K - 1:K, :] * x
            for d in range(1, K):
                xd = jnp.concatenate([wrap[nw - d * CS:, :], x[:L - d * CS, :]], axis=0)
                conv = conv + cw_ref[K - 1 - d:K - d, :] * xd
            xc = conv * jax.nn.sigmoid(conv)
            xcb = xc.astype(BF16)
            g = if_ref[:, b, :, :].reshape(L, if_ref.shape[3])
            lf = jnp.minimum(g, 0.0) - jnp.log1p(jnp.exp(-jnp.abs(g)))
            bcum = jnp.dot(tri_scr[...], lf, preferred_element_type=F32, precision=HIGHEST)
            bcum_t = bcum.T
            g_t = g.T
        st = b * H + h
        cs = slice(h * dh, (h + 1) * dh)
        q = (_dot(xcb[:, cs], wq_ref[h].astype(BF16)) * (dh ** -0.5)).astype(BF16)
        kf = _dot(xcb[:, cs], wk_ref[h].astype(BF16))
        vb = v_ref[:, b, :, cs].reshape(L, dh).astype(BF16)
        b_col = bcum[:, H + h:H + h + 1]
        i_col = g[:, h:h + 1]
        b_row = bcum_t[H + h:H + h + 1, :]
        i_row = g_t[h:h + 1, :]
        m_prev = m_scr[st, 0:1, 0:1]
        c_prev = c_scr[st]
        n_prev = n_scr[st, 0:1, :]

        dmat = b_col + (i_row - b_row) + neg_scr[...]
        inter = b_col + m_prev
        m_t = jnp.maximum(inter, jnp.max(dmat, axis=-1, keepdims=True))
        w_inter = jnp.exp(inter - m_t)
        s = _dot_nt(q, kf.astype(BF16)) * jnp.exp(dmat - m_t)
        num = w_inter * _dot(q, c_prev.astype(BF16)) + _dot(s.astype(BF16), vb)
        qn = jnp.sum(q.astype(F32) * n_prev, axis=-1, keepdims=True)
        den = w_inter * qn + jnp.sum(s, axis=-1, keepdims=True)
        ht = num / jnp.maximum(jnp.abs(den), jnp.exp(-m_t))

        g_tot = b_row[:, L - 1:L]
        dec_row = g_tot - b_row + i_row
        m_new = jnp.maximum(g_tot + m_prev, jnp.max(dec_row, axis=-1, keepdims=True))
        w_old = jnp.exp(g_tot + m_prev - m_new)
        kw = kf * jnp.exp(g_tot - b_col + i_col - m_new)
        c_scr[st] = w_old * c_prev + _dot_tn(kw.astype(BF16), vb)
        n_scr[st] = jnp.broadcast_to(w_old * n_prev + jnp.sum(kw, axis=0, keepdims=True), n_scr.shape[1:])
        m_scr[st] = jnp.broadcast_to(m_new, m_scr.shape[1:])

        hc = jax.nn.sigmoid(o_ref[:, b, :, cs].reshape(L, dh)) * ht
        hn = hc * lax.rsqrt(jnp.mean(hc * hc, axis=-1, keepdims=True) + NORM_EPS)
        y_ref[:, b, :, cs] = (hn * ng_ref[:, cs] + sk_ref[:, cs] * xc[:, cs]).reshape(T, CS, dh)


def _mlstm(xml, v, o, ifg, conv_w, conv_b, wq, wk, norm_g, skip, layer):
    T, B, ncb, W = xml.shape
    H, dh = wq.shape[1], wq.shape[2]
    K = conv_w.shape[1]
    CS = ML_CHUNK_S5
    L = T * CS

    def seq(w):
        return pl.BlockSpec((T, B, CS, w), lambda c: (0, 0, c, 0))

    return pl.pallas_call(
        _mlstm_kernel,
        grid=(ncb // CS,),
        in_specs=[seq(W), seq(W), seq(W), seq(GATE_LANES),
                  _const_spec((K, W), layer), _const_spec((1, W), layer),
                  _const_spec((H, dh, dh), layer), _const_spec((H, dh, dh), layer),
                  _const_spec((1, W), layer), _const_spec((1, W), layer)],
        out_specs=seq(W),
        out_shape=jax.ShapeDtypeStruct((T, B, ncb, W), F32),
        scratch_shapes=[pltpu.VMEM((B * H, dh, dh), F32), pltpu.VMEM((B * H, 8, dh), F32),
                        pltpu.VMEM((B * H, 8, 128), F32), pltpu.VMEM((B, (K - 1) * CS, W), F32),
                        pltpu.VMEM((L, L), F32), pltpu.VMEM((L, L), F32)],
        compiler_params=_params(("arbitrary",)),
        name="mlstm",
    )(xml, v, o, ifg, conv_w, conv_b, wq, wk, norm_g, skip)


def _gelu_tanh(x):
    return 0.5 * x * (1.0 + jnp.tanh(math.sqrt(2.0 / math.pi) * (x + 0.044715 * (x * x * x))))


def _merge_kernel(x_ref, yt_ref, ut_ref, yml_ref, g_ref, wgate_ref, d_ref, gv_ref, gg_ref,
                  wbs_ref, wbm_ref, wout_ref, o_ref):
    d = x_ref.shape[1]
    part = x_ref.shape[0] // MERGE_SPLIT
    for k in range(MERGE_SPLIT):
        tk = slice(k * part, (k + 1) * part)
        x = x_ref[tk, :]
        h = _rms(x, g_ref[...]).astype(BF16)
        z = _gelu_tanh(yt_ref[:, tk] + d_ref[...] * ut_ref[:, tk]).T.astype(BF16)
        ys5 = _dot(z, gv_ref[...].astype(BF16)) * jax.nn.sigmoid(_dot(z, gg_ref[...].astype(BF16)))
        gate_s5 = jax.nn.sigmoid(_dot_nt(h, wgate_ref[0, :d, :].astype(BF16)))
        mix = gate_s5 * _dot(ys5.astype(BF16), wbs_ref[...].astype(BF16))
        gate_ml = jax.nn.sigmoid(_dot_nt(h, wgate_ref[0, d:, :].astype(BF16)))
        mix = mix + gate_ml * _dot(yml_ref[tk, :].astype(BF16), wbm_ref[...].astype(BF16))
        o_ref[tk, :] = x + _dot(mix.astype(BF16), wout_ref[...].astype(BF16))


def _merge(x, yt, ut, yml, norm, w_in, gate_row, d_skip, gv, gg, wbs, wbm, wout, layer):
    n, d = x.shape
    T, s5w, nc = yt.shape
    mlw = yml.shape[1]

    def tok(w):
        return pl.BlockSpec((nc, w), lambda i: (i, 0))

    slab = pl.BlockSpec((None, s5w, nc), lambda i: (i, 0, 0))
    gate_rows = pl.BlockSpec((pl.Element(1), pl.Element(2 * d), pl.Element(d)),
                             lambda i: (layer, gate_row, 0), pipeline_mode=pl.Buffered(1))
    return pl.pallas_call(
        _merge_kernel,
        grid=(T,),
        in_specs=[tok(d), slab, slab, tok(mlw),
                  _const_spec((1, d), layer), gate_rows,
                  _const_spec((s5w, 1), layer), _const_spec((s5w, s5w), layer),
                  _const_spec((s5w, s5w), layer), _const_spec((s5w, d), layer),
                  _const_spec((mlw, d), layer), _const_spec((d, d), layer)],
        out_specs=tok(d),
        out_shape=jax.ShapeDtypeStruct((n, d), F32),
        compiler_params=_params(("parallel",)),
        name="merge",
    )(x, yt, ut, yml, norm, w_in, d_skip, gv, gg, wbs, wbm, wout)


def kernel(x, ffn1_norm, ffn1_wg, ffn1_wu, ffn1_wd, mix_norm, w_in, b_if, s5_lam_re, s5_lam_im, s5_log_dt, s5_b_re, s5_b_im, s5_c_re, s5_c_im, s5_d, s5_glu_v, s5_glu_g, ml_conv_w, ml_conv_b, ml_wq, ml_wk, ml_norm, ml_skip, w_br_s5, w_br_ml, w_out, ffn2_norm, ffn2_wg, ffn2_wu, ffn2_wd, final_norm):
    B, L, D = x.shape
    depth = w_in.shape[0]
    s5w = s5_d.shape[1]
    mlw = ml_norm.shape[1]
    H = ml_wq.shape[1]
    T = S5_CHUNK
    ncb = L // T
    nc = B * ncb
    assert ncb % S5_SEGMENTS == 0 and ncb % ML_CHUNK_S5 == 0 and (T * nc) % TOKEN_TILE == 0
    o3 = s5w + 3 * mlw
    o4 = o3 + 2 * H

    row = lambda a: a[:, None, :]
    ffn1 = (row(ffn1_norm), ffn1_wg, ffn1_wu, ffn1_wd)
    ffn2 = (row(ffn2_norm), ffn2_wg, ffn2_wu, ffn2_wd)
    b_ifp = row(jnp.pad(b_if, ((0, 0), (0, GATE_LANES - 2 * H))))
    w_in = jnp.swapaxes(w_in, 1, 2)
    mixn, conv_b, ml_n, ml_s = row(mix_norm), row(ml_conv_b), row(ml_norm), row(ml_skip)
    d_skip = s5_d[:, :, None]
    fin = final_norm[None, :]
    s5_params = _s5_pack(s5_lam_re, s5_lam_im, s5_log_dt, s5_b_re, s5_b_im, s5_c_re, s5_c_im)

    xt = x.reshape(B, ncb, T, D).transpose(2, 0, 1, 3).reshape(T * nc, D)
    seq4 = lambda a: a.reshape(T, B, ncb, a.shape[-1])
    for l in range(depth):
        xt = _ffn(xt, *ffn1, fin, l, False)
        ut, xml, v, o, ifg = _inproj(xt, mixn, w_in, b_ifp, l, nc, s5w, mlw, 2 * H)
        yt = _s5_scan(ut, *s5_params, l, B)
        yml = _mlstm(seq4(xml), seq4(v), seq4(o), seq4(ifg), ml_conv_w, conv_b, ml_wq, ml_wk, ml_n, ml_s, l)
        xt = _merge(xt, yt, ut, yml.reshape(T * nc, mlw), mixn, w_in, o4, d_skip, s5_glu_v, s5_glu_g,
                    w_br_s5, w_br_ml, w_out, l)
        xt = _ffn(xt, *ffn2, fin, l, l == depth - 1)
    return xt.reshape(T, B, ncb, D).transpose(1, 2, 0, 3).reshape(B, L, D)
```

```python
import functools
import math

import jax
import jax.numpy as jnp
from jax import lax
from jax.experimental import pallas as pl
from jax.experimental.pallas import tpu as pltpu

F32 = jnp.float32
BF16 = jnp.bfloat16

NORM_EPS = 1e-6
S5_GROUP = 16
S5_CHUNK = 32
S5_SEGMENTS = 8
S5_GROUPS_PER_STEP = 4
ML_HEADS = 4
ML_CHUNK_S5 = 8
GATE_LANES = 128
TOKEN_TILE = 512
FF_CHUNK = 256
MERGE_SPLIT = 2
VMEM_LIMIT = 56 * 1024 * 1024
HIGHEST = lax.Precision.HIGHEST


def _rms(x, g):
    return x * lax.rsqrt(jnp.mean(x * x, axis=-1, keepdims=True) + NORM_EPS) * g


def _dot(a, b):
    return jnp.dot(a, b, preferred_element_type=F32)


def _dot_nt(a, b):
    return lax.dot_general(a, b, (((1,), (1,)), ((), ())), preferred_element_type=F32)


def _dot_tn(a, b):
    return lax.dot_general(a, b, (((0,), (0,)), ((), ())), preferred_element_type=F32)


def _const_spec(shape, layer=None):
    nd = len(shape)
    if layer is None:
        return pl.BlockSpec(shape, lambda *_: (0,) * nd, pipeline_mode=pl.Buffered(1))
    return pl.BlockSpec((None,) + tuple(shape), lambda *_: (layer,) + (0,) * nd,
                        pipeline_mode=pl.Buffered(1))


def _params(sem):
    return pltpu.CompilerParams(dimension_semantics=sem, vmem_limit_bytes=VMEM_LIMIT)


def _ffn_kernel(x_ref, g_ref, wg_hbm, wu_hbm, wd_hbm, gf_ref, o_ref, wg_ref, wu_ref, wd_ref, sem, *,
                layer, final):
    d_ff = wg_ref.shape[1]
    n_chunks = d_ff // FF_CHUNK
    chunk = lambda c: slice(c * FF_CHUNK, (c + 1) * FF_CHUNK)

    def copies(c):
        return (pltpu.make_async_copy(wg_hbm.at[layer, :, chunk(c)], wg_ref.at[:, chunk(c)], sem.at[0, c]),
                pltpu.make_async_copy(wu_hbm.at[layer, :, chunk(c)], wu_ref.at[:, chunk(c)], sem.at[1, c]),
                pltpu.make_async_copy(wd_hbm.at[layer, chunk(c), :], wd_ref.at[chunk(c), :], sem.at[2, c]))

    def compute(first_step):
        x = x_ref[...]
        h = _rms(x, g_ref[...]).astype(BF16)
        acc = jnp.zeros(x.shape, F32)
        for c in range(n_chunks):
            if first_step:
                for cp in copies(c):
                    cp.wait()
            g = _dot(h, wg_ref[:, chunk(c)].astype(BF16))
            u = _dot(h, wu_ref[:, chunk(c)].astype(BF16))
            a = (g * jax.nn.sigmoid(g) * u).astype(BF16)
            acc = acc + _dot(a, wd_ref[chunk(c), :].astype(BF16))
        y = x + 0.5 * acc
        if final:
            y = _rms(y, gf_ref[...])
        o_ref[...] = y

    @pl.when(pl.program_id(0) == 0)
    def _():
        for c in range(n_chunks):
            for cp in copies(c):
                cp.start()
        compute(True)

    @pl.when(pl.program_id(0) != 0)
    def _():
        compute(False)


def _ffn(x, norm, wg, wu, wd, final_norm, layer, final):
    n, d = x.shape
    d_ff = wg.shape[2]
    tok = pl.BlockSpec((TOKEN_TILE, d), lambda i: (i, 0))
    hbm = pl.BlockSpec(memory_space=pl.ANY)
    return pl.pallas_call(
        functools.partial(_ffn_kernel, layer=layer, final=final),
        grid=(n // TOKEN_TILE,),
        in_specs=[tok, _const_spec((1, d), layer), hbm, hbm, hbm, _const_spec((1, d))],
        out_specs=tok,
        out_shape=jax.ShapeDtypeStruct((n, d), F32),
        scratch_shapes=[pltpu.VMEM((d, d_ff), F32), pltpu.VMEM((d, d_ff), F32), pltpu.VMEM((d_ff, d), F32),
                        pltpu.SemaphoreType.DMA((3, d_ff // FF_CHUNK))],
        compiler_params=_params(("arbitrary",)),
        name="ffn",
    )(x, norm, wg, wu, wd, final_norm)


def _inproj_kernel(x_ref, g_ref, w_ref, bif_ref, ut_ref, xml_ref, v_ref, o_ref, if_ref, *, n_gates):
    h = _rms(x_ref[...], g_ref[...]).astype(BF16)
    c0 = ut_ref.shape[0]
    c1 = c0 + xml_ref.shape[1]
    c2 = c1 + v_ref.shape[1]
    c3 = c2 + o_ref.shape[1]
    gl = if_ref.shape[1]
    ut_ref[...] = _dot_nt(w_ref[:c0, :].astype(BF16), h)
    xml_ref[...] = _dot_nt(h, w_ref[c0:c1, :].astype(BF16))
    v_ref[...] = _dot_nt(h, w_ref[c1:c2, :].astype(BF16))
    o_ref[...] = _dot_nt(h, w_ref[c2:c3, :].astype(BF16))
    gates = _dot_nt(h, w_ref[c3:c3 + gl, :].astype(BF16))
    lane = lax.broadcasted_iota(jnp.int32, gates.shape, 1)
    if_ref[...] = jnp.where(lane < n_gates, gates, 0.0) + bif_ref[...]


def _inproj(x, norm, w_in, b_if, layer, nc, s5w, mlw, n_gates):
    n, d = x.shape
    used = s5w + 3 * mlw + GATE_LANES
    assert used <= w_in.shape[1] and used % 8 == 0

    def tok(w):
        return pl.BlockSpec((nc, w), lambda i: (i, 0))

    widths = (mlw, mlw, mlw, GATE_LANES)
    return pl.pallas_call(
        functools.partial(_inproj_kernel, n_gates=n_gates),
        grid=(n // nc,),
        in_specs=[tok(d), _const_spec((1, d), layer), _const_spec((used, d), layer),
                  _const_spec((1, GATE_LANES), layer)],
        out_specs=[pl.BlockSpec((None, s5w, nc), lambda i: (i, 0, 0))] + [tok(w) for w in widths],
        out_shape=[jax.ShapeDtypeStruct((n // nc, s5w, nc), F32)]
        + [jax.ShapeDtypeStruct((n, w), F32) for w in widths],
        compiler_params=_params(("parallel",)),
        name="inproj",
    )(x, norm, w_in, b_if)


def _s5_pack(lam_re, lam_im, log_dt, b_re, b_im, c_re, c_im):
    two = lambda a: jnp.concatenate([a, a], axis=-1)
    lam = jnp.stack([two(lam_re), two(lam_im), jnp.broadcast_to(log_dt[..., None], two(lam_re).shape)]
                    + [jnp.zeros_like(two(lam_re))] * 5, axis=-2)
    bmat = jnp.concatenate([b_re, b_im], axis=-2)
    cc = jnp.concatenate([two(c_re), two(c_im)], axis=-2)
    return lam, bmat, cc


def _s5_operators(lam, b1, cc, T, P, R):
    n2 = lam.shape[1]
    N = n2 // 2
    tp = T * P
    lr = jnp.minimum(lam[0:1, :], -1e-4)
    li = lam[1:2, :]
    dt = jnp.exp(lam[2:3, :])
    lrdt = lr * dt
    lidt = li * dt
    half = lax.broadcasted_iota(jnp.int32, (1, n2), 1) < N

    def powers(k):
        pm = jnp.exp(lrdt * k)
        return pm * jnp.cos(lidt * k), pm * jnp.sin(lidt * k)

    kk = lax.broadcasted_iota(jnp.int32, (T + 8, 1), 0).astype(F32)
    pre, pim = powers(kk)
    rr = lax.broadcasted_iota(jnp.int32, (R + 8, 1), 0).astype(F32) * float(T)
    sre, sim = powers(rr)

    def forms(re, im):
        f2 = jnp.where(half, -im, im)
        return re, f2, -f2

    def to_columns(rows):
        pad = jnp.zeros((n2 - rows.shape[0], n2), F32)
        return jnp.concatenate([rows, pad], axis=0).T

    ab_re, ab_im = pre[1:2, :], pim[1:2, :]
    den = lr * lr + li * li
    q_re = ((ab_re - 1.0) * lr + ab_im * li) / den
    q_im = (ab_im * lr - (ab_re - 1.0) * li) / den
    q_col = to_columns(jnp.concatenate([q_re, q_im], axis=0))
    q_re_c, q_im_c = q_col[:, 0:1], q_col[:, 1:2]
    b2 = jnp.concatenate([-b1[N:, :], b1[:N, :]], axis=0)
    bbs = q_re_c * b1 + q_im_c * b2
    bbx = q_re_c * b2 - q_im_c * b1

    cc1, cc2 = cc[0:P, :], cc[P:2 * P, :]
    pp1 = jnp.where(half, pre, -pim)
    pp2 = jnp.where(half, -pim, -pre)

    def readout(k0):
        return jnp.concatenate([pp1[k:k + 1, :] * cc1 + pp2[k:k + 1, :] * cc2 for k in range(k0, k0 + T)], axis=0)

    v = readout(1).astype(BF16)
    kcol = jnp.dot(readout(0), bbs, preferred_element_type=F32, precision=HIGHEST).astype(BF16)

    col_i = lax.broadcasted_iota(jnp.int32, (n2, tp), 1) // P
    pick_i = jnp.where(col_i == T - 1 - lax.broadcasted_iota(jnp.int32, (n2, tp), 0), 1.0, 0.0).astype(BF16)
    col_p = lax.broadcasted_iota(jnp.int32, (P, tp), 1) % P
    pick_p = jnp.where(col_p == lax.broadcasted_iota(jnp.int32, (P, tp), 0), 1.0, 0.0).astype(BF16)

    def expand(a, e):
        hi = a.astype(BF16)
        return _dot(hi, e) + _dot((a - hi.astype(F32)).astype(BF16), e)

    bt1 = expand(bbs, pick_p)
    bt2 = jnp.concatenate([-bt1[N:, :], bt1[:N, :]], axis=0)
    w_s = expand(to_columns(pre[0:T, :]), pick_i) * bt1 + expand(to_columns(pim[0:T, :]), pick_i) * bt2
    w = jnp.concatenate([w_s, w_s[N:, :], w_s[:N, :]], axis=0).astype(BF16)

    step = forms(pre[T:T + 1, :], pim[T:T + 1, :])
    seg = forms(sre[R:R + 1, :], sim[R:R + 1, :])
    pw1, pw2, _ = forms(sre[0:R, :], sim[0:R, :])
    return kcol, w, v, step, seg, pw1, pw2


def _toeplitz(kcol, T, P):
    tp = T * P
    lane = lax.broadcasted_iota(jnp.int32, (P, tp), 1)
    sub = lax.broadcasted_iota(jnp.int32, (P, tp), 0)
    rep = jnp.where(lane % P == sub, 1.0, 0.0).astype(BF16)
    m = _dot(kcol, rep)
    blk = lax.broadcasted_iota(jnp.int32, (tp, tp), 1) // P
    shift = P
    while shift < tp:
        moved = jnp.concatenate([jnp.zeros((shift, tp), F32), m[:tp - shift, :]], axis=0)
        m = jnp.where((blk & (shift // P)) != 0, moved, m)
        shift *= 2
    return m.astype(BF16)


def _s5_kernel(u_ref, lam_ref, b_ref, cc_ref, y_ref, *, batch):
    P = S5_GROUP
    for j in range(lam_ref.shape[0]):
        rows = slice(j * P, (j + 1) * P)
        _s5_group(u_ref.at[:, rows, :], lam_ref[j], b_ref[j], cc_ref[j], y_ref.at[:, rows, :], batch)


def _s5_group(u_ref, lam, b1, cc, y_ref, batch):
    T, P, nct = u_ref.shape
    n2 = lam.shape[1]
    S = S5_SEGMENTS
    ncb = nct // batch
    R = ncb // S
    kcol, w, v, (c1, c2, c2t), (g1, g2, g2t), pw1, pw2 = _s5_operators(lam, b1, cc, T, P, R)
    u = u_ref[...].reshape(T * P, nct).astype(BF16)
    m = _toeplitz(kcol, T, P)
    x = _dot(w, u).T
    xr = [jnp.swapaxes(x[b * ncb:(b + 1) * ncb, :].reshape(S, R, 2 * n2), 0, 1).reshape(ncb, 2 * n2)
          for b in range(batch)]
    rows = lax.broadcasted_iota(jnp.int32, (S, n2), 0)

    s = [jnp.zeros((S, n2), F32)] * batch
    t = [jnp.zeros((S, n2), F32)] * batch
    local = [[] for _ in range(batch)]
    for r in range(R):
        for b in range(batch):
            local[b].append(s[b])
            xin = xr[b][r * S:(r + 1) * S, :]
            s[b], t[b] = c1 * s[b] + c2 * t[b] + xin[:, :n2], c1 * t[b] + c2t * s[b] + xin[:, n2:]
    ini_s = [jnp.zeros((S, n2), F32)] * batch
    ini_t = [jnp.zeros((S, n2), F32)] * batch
    cur_s = [jnp.zeros((1, n2), F32)] * batch
    cur_t = [jnp.zeros((1, n2), F32)] * batch
    for k in range(1, S):
        for b in range(batch):
            cur_s[b], cur_t[b] = (g1 * cur_s[b] + g2 * cur_t[b] + s[b][k - 1:k, :],
                                  g1 * cur_t[b] + g2t * cur_s[b] + t[b][k - 1:k, :])
            ini_s[b] = jnp.where(rows == k, cur_s[b], ini_s[b])
            ini_t[b] = jnp.where(rows == k, cur_t[b], ini_t[b])
    before = []
    for b in range(batch):
        sb = jnp.concatenate([local[b][r] + pw1[r:r + 1, :] * ini_s[b] + pw2[r:r + 1, :] * ini_t[b]
                              for r in range(R)], axis=0)
        before.append(jnp.swapaxes(sb.reshape(R, S, n2), 0, 1).reshape(ncb, n2))
    y = _dot(m, u) + _dot_nt(v, jnp.concatenate(before, axis=0).astype(BF16))
    y_ref[...] = y.reshape(T, P, nct)


def _s5_scan(ut, lam, bmat, cc, layer, batch):
    T, s5w, nct = ut.shape
    G = s5w // S5_GROUP
    gs = S5_GROUPS_PER_STEP

    def grp(a):
        return pl.BlockSpec((None, gs) + a.shape[2:], lambda g: (layer, g, 0, 0))

    slab = pl.BlockSpec((T, gs * S5_GROUP, nct), lambda g: (0, g, 0))
    return pl.pallas_call(
        functools.partial(_s5_kernel, batch=batch),
        grid=(G // gs,),
        in_specs=[slab, grp(lam), grp(bmat), grp(cc)],
        out_specs=slab,
        out_shape=jax.ShapeDtypeStruct((T, s5w, nct), F32),
        compiler_params=_params(("parallel",)),
        name="s5_scan",
    )(ut, lam, bmat, cc)


def _mlstm_kernel(xml_ref, v_ref, o_ref, if_ref, cw_ref, cb_ref, wq_ref, wk_ref, ng_ref, sk_ref,
                  y_ref, c_scr, n_scr, m_scr, tail_scr, tri_scr, neg_scr):
    H = wq_ref.shape[0]
    dh = wq_ref.shape[1]
    T, B, CS, W = xml_ref.shape
    L = T * CS
    K = cw_ref.shape[0]

    @pl.when(pl.program_id(0) == 0)
    def _():
        c_scr[...] = jnp.zeros(c_scr.shape, F32)
        n_scr[...] = jnp.zeros(n_scr.shape, F32)
        m_scr[...] = jnp.zeros(m_scr.shape, F32)
        tail_scr[...] = jnp.zeros(tail_scr.shape, F32)
        ri = lax.broadcasted_iota(jnp.int32, (L, L), 0)
        ci = lax.broadcasted_iota(jnp.int32, (L, L), 1)
        causal = ((ci % CS) * T + ci // CS) <= ((ri % CS) * T + ri // CS)
        tri_scr[...] = jnp.where(causal, 1.0, 0.0)
        neg_scr[...] = jnp.where(causal, 0.0, -jnp.inf)

    nw = (K - 1) * CS
    crow = lax.broadcasted_iota(jnp.int32, (nw, W), 0) % CS
    ones = jnp.ones((L, n_scr.shape[2]), BF16)
    for b, h in [(b, h) for b in range(B) for h in range(H)]:
        if h == 0:
            x = xml_ref[:, b, :, :].reshape(L, W)
            last = x[L - nw:, :]
            wrap = jnp.where(crow == 0, pltpu.roll(tail_scr[b], nw - (CS - 1), 0), pltpu.roll(last, 1, 0))
            tail_scr[b] = last
            conv = cb_ref[...] + cw_ref[K - 1:K, :] * x
            for d in range(1, K):
                xd = jnp.concatenate([wrap[nw - d * CS:, :], x[:L - d * CS, :]], axis=0)
                conv = conv + cw_ref[K - 1 - d:K - d, :] * xd
            xc = conv * jax.nn.sigmoid(conv)
            xcb = xc.astype(BF16)
            g = if_ref[:, b, :, :].reshape(L, if_ref.shape[3])
            lf = jnp.minimum(g, 0.0) - jnp.log1p(jnp.exp(-jnp.abs(g)))
            bcum = jnp.dot(tri_scr[...], lf, preferred_element_type=F32, precision=HIGHEST)
            bcum_t = bcum.T
            g_t = g.T
        st = b * H + h
        cs = slice(h * dh, (h + 1) * dh)
        q32 = _dot(xcb[:, cs], wq_ref[h].astype(BF16)) * (dh ** -0.5)
        q = q32.astype(BF16)
        kf = _dot(xcb[:, cs], wk_ref[h].astype(BF16))
        v1 = jnp.concatenate([v_ref[:, b, :, cs].reshape(L, dh).astype(BF16), ones], axis=1)
        b_col = bcum[:, H + h:H + h + 1]
        r_col = g[:, h:h + 1] - b_col
        r_row = g_t[h:h + 1, :] - bcum_t[H + h:H + h + 1, :]
        m_prev = m_scr[st, 0:1, 0:1]
        c_prev = c_scr[st]
        n_prev = n_scr[st]

        arg = r_row + neg_scr[...]
        mu = jnp.maximum(m_prev, jnp.max(arg, axis=-1, keepdims=True))
        w_inter = jnp.exp(m_prev - mu)
        s = _dot_nt(q, kf.astype(BF16)) * jnp.exp(arg - mu)
        lhs = jnp.concatenate([s.astype(BF16), (q32 * w_inter).astype(BF16)], axis=1)
        rhs = jnp.concatenate([v1, jnp.concatenate([c_prev.astype(BF16), n_prev.astype(BF16)], axis=1)], axis=0)
        nd = _dot(lhs, rhs)
        rden = 1.0 / jnp.maximum(jnp.abs(nd[:, dh:]), jnp.exp(-(b_col + mu)))
        ht = nd[:, :dh] * jnp.concatenate([rden] * (dh // rden.shape[1]), axis=1)

        mu_end = jnp.maximum(m_prev, jnp.max(r_row, axis=-1, keepdims=True))
        w_old = jnp.exp(m_prev - mu_end)
        kw = kf * jnp.exp(r_col - mu_end)
        upd = _dot_tn(kw.astype(BF16), v1)
        c_scr[st] = w_old * c_prev + upd[:, :dh]
        n_scr[st] = w_old * n_prev + upd[:, dh:]
        m_scr[st] = jnp.broadcast_to(bcum_t[H + h:H + h + 1, L - 1:L] + mu_end, m_scr.shape[1:])

        hc = jax.nn.sigmoid(o_ref[:, b, :, cs].reshape(L, dh)) * ht
        hn = hc * lax.rsqrt(jnp.mean(hc * hc, axis=-1, keepdims=True) + NORM_EPS)
        y_ref[:, b, :, cs] = (hn * ng_ref[:, cs] + sk_ref[:, cs] * xc[:, cs]).reshape(T, CS, dh)


def _mlstm(xml, v, o, ifg, conv_w, conv_b, wq, wk, norm_g, skip, layer):
    T, B, ncb, W = xml.shape
    H, dh = wq.shape[1], wq.shape[2]
    K = conv_w.shape[1]
    CS = ML_CHUNK_S5
    L = T * CS

    def seq(w):
        return pl.BlockSpec((T, B, CS, w), lambda c: (0, 0, c, 0))

    return pl.pallas_call(
        _mlstm_kernel,
        grid=(ncb // CS,),
        in_specs=[seq(W), seq(W), seq(W), seq(GATE_LANES),
                  _const_spec((K, W), layer), _const_spec((1, W), layer),
                  _const_spec((H, dh, dh), layer), _const_spec((H, dh, dh), layer),
                  _const_spec((1, W), layer), _const_spec((1, W), layer)],
        out_specs=seq(W),
        out_shape=jax.ShapeDtypeStruct((T, B, ncb, W), F32),
        scratch_shapes=[pltpu.VMEM((B * H, dh, dh), F32), pltpu.VMEM((B * H, dh, 128), F32),
                        pltpu.VMEM((B * H, 8, 128), F32), pltpu.VMEM((B, (K - 1) * CS, W), F32),
                        pltpu.VMEM((L, L), F32), pltpu.VMEM((L, L), F32)],
        compiler_params=_params(("arbitrary",)),
        name="mlstm",
    )(xml, v, o, ifg, conv_w, conv_b, wq, wk, norm_g, skip)


def _gelu_tanh(x):
    return 0.5 * x * (1.0 + jnp.tanh(math.sqrt(2.0 / math.pi) * (x + 0.044715 * (x * x * x))))


def _merge_kernel(x_ref, yt_ref, ut_ref, yml_ref, g_ref, wgate_ref, d_ref, gv_ref, gg_ref,
                  wbs_ref, wbm_ref, wout_ref, o_ref):
    d = x_ref.shape[1]
    part = x_ref.shape[0] // MERGE_SPLIT
    for k in range(MERGE_SPLIT):
        tk = slice(k * part, (k + 1) * part)
        x = x_ref[tk, :]
        h = _rms(x, g_ref[...]).astype(BF16)
        z = _gelu_tanh(yt_ref[:, tk] + d_ref[...] * ut_ref[:, tk]).T.astype(BF16)
        ys5 = _dot(z, gv_ref[...].astype(BF16)) * jax.nn.sigmoid(_dot(z, gg_ref[...].astype(BF16)))
        gate_s5 = jax.nn.sigmoid(_dot_nt(h, wgate_ref[0, :d, :].astype(BF16)))
        mix = gate_s5 * _dot(ys5.astype(BF16), wbs_ref[...].astype(BF16))
        gate_ml = jax.nn.sigmoid(_dot_nt(h, wgate_ref[0, d:, :].astype(BF16)))
        mix = mix + gate_ml * _dot(yml_ref[tk, :].astype(BF16), wbm_ref[...].astype(BF16))
        o_ref[tk, :] = x + _dot(mix.astype(BF16), wout_ref[...].astype(BF16))


def _merge(x, yt, ut, yml, norm, w_in, gate_row, d_skip, gv, gg, wbs, wbm, wout, layer):
    n, d = x.shape
    T, s5w, nc = yt.shape
    mlw = yml.shape[1]

    def tok(w):
        return pl.BlockSpec((nc, w), lambda i: (i, 0))

    slab = pl.BlockSpec((None, s5w, nc), lambda i: (i, 0, 0))
    gate_rows = pl.BlockSpec((pl.Element(1), pl.Element(2 * d), pl.Element(d)),
                             lambda i: (layer, gate_row, 0), pipeline_mode=pl.Buffered(1))
    return pl.pallas_call(
        _merge_kernel,
        grid=(T,),
        in_specs=[tok(d), slab, slab, tok(mlw),
                  _const_spec((1, d), layer), gate_rows,
                  _const_spec((s5w, 1), layer), _const_spec((s5w, s5w), layer),
                  _const_spec((s5w, s5w), layer), _const_spec((s5w, d), layer),
                  _const_spec((mlw, d), layer), _const_spec((d, d), layer)],
        out_specs=tok(d),
        out_shape=jax.ShapeDtypeStruct((n, d), F32),
        compiler_params=_params(("parallel",)),
        name="merge",
    )(x, yt, ut, yml, norm, w_in, d_skip, gv, gg, wbs, wbm, wout)


def kernel(x, ffn1_norm, ffn1_wg, ffn1_wu, ffn1_wd, mix_norm, w_in, b_if, s5_lam_re, s5_lam_im, s5_log_dt, s5_b_re, s5_b_im, s5_c_re, s5_c_im, s5_d, s5_glu_v, s5_glu_g, ml_conv_w, ml_conv_b, ml_wq, ml_wk, ml_norm, ml_skip, w_br_s5, w_br_ml, w_out, ffn2_norm, ffn2_wg, ffn2_wu, ffn2_wd, final_norm):
    B, L, D = x.shape
    depth = w_in.shape[0]
    s5w = s5_d.shape[1]
    mlw = ml_norm.shape[1]
    H = ml_wq.shape[1]
    T = S5_CHUNK
    ncb = L // T
    nc = B * ncb
    assert ncb % S5_SEGMENTS == 0 and ncb % ML_CHUNK_S5 == 0 and (T * nc) % TOKEN_TILE == 0
    o3 = s5w + 3 * mlw
    o4 = o3 + 2 * H

    row = lambda a: a[:, None, :]
    ffn1 = (row(ffn1_norm), ffn1_wg, ffn1_wu, ffn1_wd)
    ffn2 = (row(ffn2_norm), ffn2_wg, ffn2_wu, ffn2_wd)
    b_ifp = row(jnp.pad(b_if, ((0, 0), (0, GATE_LANES - 2 * H))))
    w_in = jnp.swapaxes(w_in, 1, 2)
    mixn, conv_b, ml_n, ml_s = row(mix_norm), row(ml_conv_b), row(ml_norm), row(ml_skip)
    d_skip = s5_d[:, :, None]
    fin = final_norm[None, :]
    s5_params = _s5_pack(s5_lam_re, s5_lam_im, s5_log_dt, s5_b_re, s5_b_im, s5_c_re, s5_c_im)

    xt = x.reshape(B, ncb, T, D).transpose(2, 0, 1, 3).reshape(T * nc, D)
    seq4 = lambda a: a.reshape(T, B, ncb, a.shape[-1])
    for l in range(depth):
        xt = _ffn(xt, *ffn1, fin, l, False)
        ut, xml, v, o, ifg = _inproj(xt, mixn, w_in, b_ifp, l, nc, s5w, mlw, 2 * H)
        yt = _s5_scan(ut, *s5_params, l, B)
        yml = _mlstm(seq4(xml), seq4(v), seq4(o), seq4(ifg), ml_conv_w, conv_b, ml_wq, ml_wk, ml_n, ml_s, l)
        xt = _merge(xt, yt, ut, yml.reshape(T * nc, mlw), mixn, w_in, o4, d_skip, s5_glu_v, s5_glu_g,
                    w_br_s5, w_br_ml, w_out, l)
        xt = _ffn(xt, *ffn2, fin, l, l == depth - 1)
    return xt.reshape(T, B, ncb, D).transpose(1, 2, 0, 3).reshape(B, L, D)
```

```python
import functools
import math

import jax
import jax.numpy as jnp
from jax import lax
from jax.experimental import pallas as pl
from jax.experimental.pallas import tpu as pltpu

F32 = jnp.float32
BF16 = jnp.bfloat16

NORM_EPS = 1e-6
S5_GROUP = 16
S5_CHUNK = 32
S5_SEGMENTS = 8
S5_GROUPS_PER_STEP = 4
ML_HEADS = 4
ML_CHUNK_S5 = 8
LANES = 128
SUBLANES = 8
GATE_LANES = LANES
TOKEN_TILE = 512
FF_CHUNK = 256
MERGE_SPLIT = 2
VMEM_LIMIT = 56 * 1024 * 1024
HIGHEST = lax.Precision.HIGHEST


def _rms(x, g):
    return x * lax.rsqrt(jnp.mean(x * x, axis=-1, keepdims=True) + NORM_EPS) * g


def _dot(a, b):
    return jnp.dot(a, b, preferred_element_type=F32)


def _dot_nt(a, b):
    return lax.dot_general(a, b, (((1,), (1,)), ((), ())), preferred_element_type=F32)


def _dot_tn(a, b):
    return lax.dot_general(a, b, (((0,), (0,)), ((), ())), preferred_element_type=F32)


def _const_spec(shape, layer=None):
    nd = len(shape)
    if layer is None:
        return pl.BlockSpec(shape, lambda *_: (0,) * nd, pipeline_mode=pl.Buffered(1))
    return pl.BlockSpec((None,) + tuple(shape), lambda *_: (layer,) + (0,) * nd,
                        pipeline_mode=pl.Buffered(1))


def _params(sem):
    return pltpu.CompilerParams(dimension_semantics=sem, vmem_limit_bytes=VMEM_LIMIT)


def _ffn_kernel(x_ref, g_ref, wg_hbm, wu_hbm, wd_hbm, gf_ref, o_ref, wg_ref, wu_ref, wd_ref, sem, *,
                layer, final):
    d_ff = wg_ref.shape[1]
    n_chunks = d_ff // FF_CHUNK
    chunk = lambda c: slice(c * FF_CHUNK, (c + 1) * FF_CHUNK)

    def copies(c):
        return (pltpu.make_async_copy(wg_hbm.at[layer, :, chunk(c)], wg_ref.at[:, chunk(c)], sem.at[0, c]),
                pltpu.make_async_copy(wu_hbm.at[layer, :, chunk(c)], wu_ref.at[:, chunk(c)], sem.at[1, c]),
                pltpu.make_async_copy(wd_hbm.at[layer, chunk(c), :], wd_ref.at[chunk(c), :], sem.at[2, c]))

    def compute(first_step):
        x = x_ref[...]
        h = _rms(x, g_ref[...]).astype(BF16)
        acc = jnp.zeros(x.shape, F32)
        for c in range(n_chunks):
            if first_step:
                for cp in copies(c):
                    cp.wait()
            g = _dot(h, wg_ref[:, chunk(c)].astype(BF16))
            u = _dot(h, wu_ref[:, chunk(c)].astype(BF16))
            a = (g * jax.nn.sigmoid(g) * u).astype(BF16)
            acc = acc + _dot(a, wd_ref[chunk(c), :].astype(BF16))
        y = x + 0.5 * acc
        if final:
            y = _rms(y, gf_ref[...])
        o_ref[...] = y

    @pl.when(pl.program_id(0) == 0)
    def _():
        for c in range(n_chunks):
            for cp in copies(c):
                cp.start()
        compute(True)

    @pl.when(pl.program_id(0) != 0)
    def _():
        compute(False)


def _ffn(x, norm, wg, wu, wd, final_norm, layer, final):
    n, d = x.shape
    d_ff = wg.shape[2]
    tok = pl.BlockSpec((TOKEN_TILE, d), lambda i: (i, 0))
    hbm = pl.BlockSpec(memory_space=pl.ANY)
    return pl.pallas_call(
        functools.partial(_ffn_kernel, layer=layer, final=final),
        grid=(n // TOKEN_TILE,),
        in_specs=[tok, _const_spec((1, d), layer), hbm, hbm, hbm, _const_spec((1, d))],
        out_specs=tok,
        out_shape=jax.ShapeDtypeStruct((n, d), F32),
        scratch_shapes=[pltpu.VMEM((d, d_ff), F32), pltpu.VMEM((d, d_ff), F32), pltpu.VMEM((d_ff, d), F32),
                        pltpu.SemaphoreType.DMA((3, d_ff // FF_CHUNK))],
        compiler_params=_params(("arbitrary",)),
        name="ffn",
    )(x, norm, wg, wu, wd, final_norm)


def _inproj_kernel(x_ref, g_ref, w_ref, bif_ref, ut_ref, xml_ref, v_ref, o_ref, if_ref, *, n_gates):
    h = _rms(x_ref[...], g_ref[...]).astype(BF16)
    c0 = ut_ref.shape[0]
    c1 = c0 + xml_ref.shape[1]
    c2 = c1 + v_ref.shape[1]
    c3 = c2 + o_ref.shape[1]
    gl = if_ref.shape[1]
    ut_ref[...] = _dot_nt(w_ref[:c0, :].astype(BF16), h)
    xml_ref[...] = _dot_nt(h, w_ref[c0:c1, :].astype(BF16))
    v_ref[...] = _dot_nt(h, w_ref[c1:c2, :].astype(BF16))
    o_ref[...] = _dot_nt(h, w_ref[c2:c3, :].astype(BF16))
    gates = _dot_nt(h, w_ref[c3:c3 + gl, :].astype(BF16))
    lane = lax.broadcasted_iota(jnp.int32, gates.shape, 1)
    if_ref[...] = jnp.where(lane < n_gates, gates, 0.0) + bif_ref[...]


def _inproj(x, norm, w_in, b_if, layer, nc, s5w, mlw, n_gates):
    n, d = x.shape
    used = s5w + 3 * mlw + GATE_LANES
    assert used <= w_in.shape[1] and used % 8 == 0

    def tok(w):
        return pl.BlockSpec((nc, w), lambda i: (i, 0))

    widths = (mlw, mlw, mlw, GATE_LANES)
    return pl.pallas_call(
        functools.partial(_inproj_kernel, n_gates=n_gates),
        grid=(n // nc,),
        in_specs=[tok(d), _const_spec((1, d), layer), _const_spec((used, d), layer),
                  _const_spec((1, GATE_LANES), layer)],
        out_specs=[pl.BlockSpec((None, s5w, nc), lambda i: (i, 0, 0))] + [tok(w) for w in widths],
        out_shape=[jax.ShapeDtypeStruct((n // nc, s5w, nc), F32)]
        + [jax.ShapeDtypeStruct((n, w), F32) for w in widths],
        compiler_params=_params(("parallel",)),
        name="inproj",
    )(x, norm, w_in, b_if)


def _s5_pack(lam_re, lam_im, log_dt, b_re, b_im, c_re, c_im):
    two = lambda a: jnp.concatenate([a, a], axis=-1)
    lam = jnp.stack([two(lam_re), two(lam_im), jnp.broadcast_to(log_dt[..., None], two(lam_re).shape)]
                    + [jnp.zeros_like(two(lam_re))] * 5, axis=-2)
    bmat = jnp.concatenate([jnp.swapaxes(b_re, -1, -2), jnp.swapaxes(b_im, -1, -2)], axis=-1)
    cc = jnp.concatenate([two(c_re), two(c_im)], axis=-2)
    return lam, bmat, cc


def _s5_operators(lam, b1t, cc, T, P, R):
    n2 = lam.shape[1]
    N = n2 // 2
    tp = T * P
    lr = jnp.minimum(lam[0:1, :], -1e-4)
    li = lam[1:2, :]
    dt = jnp.exp(lam[2:3, :])
    lrdt = lr * dt
    lidt = li * dt
    half = lax.broadcasted_iota(jnp.int32, (1, n2), 1) < N

    def powers(k):
        pm = jnp.exp(lrdt * k)
        return pm * jnp.cos(lidt * k), pm * jnp.sin(lidt * k)

    kk = lax.broadcasted_iota(jnp.int32, (T + 8, 1), 0).astype(F32)
    pre, pim = powers(kk)
    rr = lax.broadcasted_iota(jnp.int32, (R + 8, 1), 0).astype(F32) * float(T)
    sre, sim = powers(rr)

    def forms(re, im):
        f2 = jnp.where(half, -im, im)
        return re, f2, -f2

    def to_columns(rows):
        pad = jnp.zeros((n2 - rows.shape[0], n2), F32)
        return jnp.concatenate([rows, pad], axis=0).T

    ab_re, ab_im = pre[1:2, :], pim[1:2, :]
    den = lr * lr + li * li
    q_re = ((ab_re - 1.0) * lr + ab_im * li) / den
    q_im = (ab_im * lr - (ab_re - 1.0) * li) / den
    swapped = pltpu.roll(b1t, N, 1)
    b2t = jnp.where(half, -swapped, swapped)
    bbs_t = q_re * b1t + q_im * b2t

    cc1, cc2 = cc[0:P, :], cc[P:2 * P, :]
    pp1 = jnp.where(half, pre, -pim)
    pp2 = jnp.where(half, -pim, -pre)

    def readout(k0):
        return jnp.concatenate([pp1[k:k + 1, :] * cc1 + pp2[k:k + 1, :] * cc2 for k in range(k0, k0 + T)], axis=0)

    v = readout(1).astype(BF16)
    kcol_t = lax.dot_general(bbs_t, readout(0), (((1,), (1,)), ((), ())), preferred_element_type=F32,
                             precision=HIGHEST).astype(BF16)

    col_i = lax.broadcasted_iota(jnp.int32, (n2, tp), 1) // P
    pick_i = jnp.where(col_i == T - 1 - lax.broadcasted_iota(jnp.int32, (n2, tp), 0), 1.0, 0.0).astype(BF16)
    col_p = lax.broadcasted_iota(jnp.int32, (P, tp), 1) % P
    pick_p = jnp.where(col_p == lax.broadcasted_iota(jnp.int32, (P, tp), 0), 1.0, 0.0).astype(BF16)

    def split(a):
        hi = a.astype(BF16)
        return hi, (a - hi.astype(F32)).astype(BF16)

    def expand(a, e):
        hi, lo = split(a)
        return _dot(hi, e) + _dot(lo, e)

    b_hi, b_lo = split(bbs_t)
    bt1 = _dot_tn(b_hi, pick_p) + _dot_tn(b_lo, pick_p)
    bt2 = jnp.concatenate([-bt1[N:, :], bt1[:N, :]], axis=0)
    w_s = expand(to_columns(pre[0:T, :]), pick_i) * bt1 + expand(to_columns(pim[0:T, :]), pick_i) * bt2
    w = jnp.concatenate([w_s, w_s[N:, :], w_s[:N, :]], axis=0).astype(BF16)

    step = forms(pre[T:T + 1, :], pim[T:T + 1, :])
    seg = forms(sre[R:R + 1, :], sim[R:R + 1, :])
    pw1, pw2, _ = forms(sre[0:R, :], sim[0:R, :])
    return kcol_t, w, v, step, seg, pw1, pw2


def _toeplitz(kcol_t, T, P):
    tp = T * P
    lane = lax.broadcasted_iota(jnp.int32, (P, tp), 1)
    sub = lax.broadcasted_iota(jnp.int32, (P, tp), 0)
    rep = jnp.where(lane % P == sub, 1.0, 0.0).astype(BF16)
    m = _dot_tn(kcol_t, rep)
    blk = lax.broadcasted_iota(jnp.int32, (tp, tp), 1) // P
    shift = P
    while shift < tp:
        moved = jnp.concatenate([jnp.zeros((shift, tp), F32), m[:tp - shift, :]], axis=0)
        m = jnp.where((blk & (shift // P)) != 0, moved, m)
        shift *= 2
    return m.astype(BF16)


def _s5_kernel(u_ref, lam_ref, b_ref, cc_ref, y_ref, *, batch):
    P = S5_GROUP
    for j in range(lam_ref.shape[0]):
        rows = slice(j * P, (j + 1) * P)
        _s5_group(u_ref.at[:, rows, :], lam_ref[j], b_ref[j], cc_ref[j], y_ref.at[:, rows, :], batch)


def _s5_group(u_ref, lam, b1t, cc, y_ref, batch):
    T, P, nct = u_ref.shape
    n2 = lam.shape[1]
    S = S5_SEGMENTS
    ncb = nct // batch
    R = ncb // S
    kcol_t, w, v, (c1, c2, c2t), (g1, g2, g2t), pw1, pw2 = _s5_operators(lam, b1t, cc, T, P, R)
    u = u_ref[...].reshape(T * P, nct).astype(BF16)
    m = _toeplitz(kcol_t, T, P)
    x = _dot(w, u).T
    xr = [jnp.swapaxes(x[b * ncb:(b + 1) * ncb, :].reshape(S, R, 2 * n2), 0, 1).reshape(ncb, 2 * n2)
          for b in range(batch)]
    rows = lax.broadcasted_iota(jnp.int32, (S, n2), 0)

    s = [jnp.zeros((S, n2), F32)] * batch
    t = [jnp.zeros((S, n2), F32)] * batch
    local = [[] for _ in range(batch)]
    for r in range(R):
        for b in range(batch):
            local[b].append(s[b])
            xin = xr[b][r * S:(r + 1) * S, :]
            s[b], t[b] = c1 * s[b] + c2 * t[b] + xin[:, :n2], c1 * t[b] + c2t * s[b] + xin[:, n2:]
    ini_s = [jnp.zeros((S, n2), F32)] * batch
    ini_t = [jnp.zeros((S, n2), F32)] * batch
    cur_s = [jnp.zeros((1, n2), F32)] * batch
    cur_t = [jnp.zeros((1, n2), F32)] * batch
    for k in range(1, S):
        for b in range(batch):
            cur_s[b], cur_t[b] = (g1 * cur_s[b] + g2 * cur_t[b] + s[b][k - 1:k, :],
                                  g1 * cur_t[b] + g2t * cur_s[b] + t[b][k - 1:k, :])
            ini_s[b] = jnp.where(rows == k, cur_s[b], ini_s[b])
            ini_t[b] = jnp.where(rows == k, cur_t[b], ini_t[b])
    before = []
    for b in range(batch):
        sb = jnp.concatenate([local[b][r] + pw1[r:r + 1, :] * ini_s[b] + pw2[r:r + 1, :] * ini_t[b]
                              for r in range(R)], axis=0)
        before.append(jnp.swapaxes(sb.reshape(R, S, n2), 0, 1).reshape(ncb, n2))
    y = _dot(m, u) + _dot_nt(v, jnp.concatenate(before, axis=0).astype(BF16))
    y_ref[...] = y.reshape(T, P, nct)


def _s5_scan(ut, lam, bmat, cc, layer, batch):
    T, s5w, nct = ut.shape
    G = s5w // S5_GROUP
    gs = S5_GROUPS_PER_STEP

    def grp(a):
        return pl.BlockSpec((None, gs) + a.shape[2:], lambda g: (layer, g, 0, 0))

    slab = pl.BlockSpec((T, gs * S5_GROUP, nct), lambda g: (0, g, 0))
    return pl.pallas_call(
        functools.partial(_s5_kernel, batch=batch),
        grid=(G // gs,),
        in_specs=[slab, grp(lam), grp(bmat), grp(cc)],
        out_specs=slab,
        out_shape=jax.ShapeDtypeStruct((T, s5w, nct), F32),
        compiler_params=_params(("parallel",)),
        name="s5_scan",
    )(ut, lam, bmat, cc)


def _mlstm_kernel(xml_ref, v_ref, o_ref, if_ref, cw_ref, cb_ref, wq_ref, wk_ref, ng_ref, sk_ref,
                  y_ref, c_scr, n_scr, m_scr, tail_scr, tri_scr, neg_scr):
    H = wq_ref.shape[0]
    dh = wq_ref.shape[1]
    T, B, CS, W = xml_ref.shape
    L = T * CS
    K = cw_ref.shape[0]

    @pl.when(pl.program_id(0) == 0)
    def _():
        c_scr[...] = jnp.zeros(c_scr.shape, F32)
        n_scr[...] = jnp.zeros(n_scr.shape, F32)
        m_scr[...] = jnp.zeros(m_scr.shape, F32)
        tail_scr[...] = jnp.zeros(tail_scr.shape, F32)
        ri = lax.broadcasted_iota(jnp.int32, (L, L), 0)
        ci = lax.broadcasted_iota(jnp.int32, (L, L), 1)
        causal = ((ci % CS) * T + ci // CS) <= ((ri % CS) * T + ri // CS)
        tri_scr[...] = jnp.where(causal, 1.0, 0.0)
        neg_scr[...] = jnp.where(causal, 0.0, -jnp.inf)

    nw = (K - 1) * CS
    crow = lax.broadcasted_iota(jnp.int32, (nw, W), 0) % CS
    ones = jnp.ones((L, n_scr.shape[2]), BF16)
    for b, h in [(b, h) for b in range(B) for h in range(H)]:
        if h == 0:
            x = xml_ref[:, b, :, :].reshape(L, W)
            last = x[L - nw:, :]
            wrap = jnp.where(crow == 0, pltpu.roll(tail_scr[b], nw - (CS - 1), 0), pltpu.roll(last, 1, 0))
            tail_scr[b] = last
            conv = cb_ref[...] + cw_ref[K - 1:K, :] * x
            for d in range(1, K):
                xd = jnp.concatenate([wrap[nw - d * CS:, :], x[:L - d * CS, :]], axis=0)
                conv = conv + cw_ref[K - 1 - d:K - d, :] * xd
            xc = conv * jax.nn.sigmoid(conv)
            xcb = xc.astype(BF16)
            g = if_ref[:, b, :, :].reshape(L, if_ref.shape[3])
            lf = jnp.minimum(g, 0.0) - jnp.log1p(jnp.exp(-jnp.abs(g)))
            bcum = jnp.dot(tri_scr[...], lf, preferred_element_type=F32, precision=HIGHEST)
            bcum_t = bcum.T
            g_t = g.T
        st = b * H + h
        cs = slice(h * dh, (h + 1) * dh)
        q32 = _dot(xcb[:, cs], wq_ref[h].astype(BF16)) * (dh ** -0.5)
        q = q32.astype(BF16)
        kf = _dot(xcb[:, cs], wk_ref[h].astype(BF16))
        v1 = jnp.concatenate([v_ref[:, b, :, cs].reshape(L, dh).astype(BF16), ones], axis=1)
        b_col = bcum[:, H + h:H + h + 1]
        r_col = g[:, h:h + 1] - b_col
        r_row = g_t[h:h + 1, :] - bcum_t[H + h:H + h + 1, :]
        m_prev = m_scr[st, 0:1, 0:1]
        c_prev = c_scr[st]
        n_prev = n_scr[st]

        arg = r_row + neg_scr[...]
        mu = jnp.maximum(m_prev, jnp.max(arg, axis=-1, keepdims=True))
        w_inter = jnp.exp(m_prev - mu)
        s = _dot_nt(q, kf.astype(BF16)) * jnp.exp(arg - mu)
        lhs = jnp.concatenate([s.astype(BF16), (q32 * w_inter).astype(BF16)], axis=1)
        rhs = jnp.concatenate([v1, jnp.concatenate([c_prev.astype(BF16), n_prev.astype(BF16)], axis=1)], axis=0)
        nd = _dot(lhs, rhs)
        rden = 1.0 / jnp.maximum(jnp.abs(nd[:, dh:]), jnp.exp(-(b_col + mu)))
        ht = nd[:, :dh] * jnp.concatenate([rden] * (dh // rden.shape[1]), axis=1)

        mu_end = jnp.maximum(m_prev, jnp.max(r_row, axis=-1, keepdims=True))
        w_old = jnp.exp(m_prev - mu_end)
        kw = kf * jnp.exp(r_col - mu_end)
        upd = _dot_tn(kw.astype(BF16), v1)
        c_scr[st] = w_old * c_prev + upd[:, :dh]
        n_scr[st] = w_old * n_prev + upd[:, dh:]
        m_scr[st] = jnp.broadcast_to(bcum_t[H + h:H + h + 1, L - 1:L] + mu_end, m_scr.shape[1:])

        hc = jax.nn.sigmoid(o_ref[:, b, :, cs].reshape(L, dh)) * ht
        hn = hc * lax.rsqrt(jnp.mean(hc * hc, axis=-1, keepdims=True) + NORM_EPS)
        y_ref[:, b, :, cs] = (hn * ng_ref[:, cs] + sk_ref[:, cs] * xc[:, cs]).reshape(T, CS, dh)


def _mlstm(xml, v, o, ifg, conv_w, conv_b, wq, wk, norm_g, skip, layer):
    T, B, ncb, W = xml.shape
    H, dh = wq.shape[1], wq.shape[2]
    K = conv_w.shape[1]
    CS = ML_CHUNK_S5
    L = T * CS

    def seq(w):
        return pl.BlockSpec((T, B, CS, w), lambda c: (0, 0, c, 0))

    return pl.pallas_call(
        _mlstm_kernel,
        grid=(ncb // CS,),
        in_specs=[seq(W), seq(W), seq(W), seq(GATE_LANES),
                  _const_spec((K, W), layer), _const_spec((1, W), layer),
                  _const_spec((H, dh, dh), layer), _const_spec((H, dh, dh), layer),
                  _const_spec((1, W), layer), _const_spec((1, W), layer)],
        out_specs=seq(W),
        out_shape=jax.ShapeDtypeStruct((T, B, ncb, W), F32),
        scratch_shapes=[pltpu.VMEM((B * H, dh, dh), F32), pltpu.VMEM((B * H, dh, LANES), F32),
                        pltpu.VMEM((B * H, SUBLANES, LANES), F32), pltpu.VMEM((B, (K - 1) * CS, W), F32),
                        pltpu.VMEM((L, L), F32), pltpu.VMEM((L, L), F32)],
        compiler_params=_params(("arbitrary",)),
        name="mlstm",
    )(xml, v, o, ifg, conv_w, conv_b, wq, wk, norm_g, skip)


def _gelu_tanh(x):
    return 0.5 * x * (1.0 + jnp.tanh(math.sqrt(2.0 / math.pi) * (x + 0.044715 * (x * x * x))))


def _merge_kernel(x_ref, yt_ref, ut_ref, yml_ref, g_ref, wgate_ref, d_ref, gv_ref, gg_ref,
                  wbs_ref, wbm_ref, wout_ref, o_ref):
    d = x_ref.shape[1]
    part = x_ref.shape[0] // MERGE_SPLIT
    for k in range(MERGE_SPLIT):
        tk = slice(k * part, (k + 1) * part)
        x = x_ref[tk, :]
        h = _rms(x, g_ref[...]).astype(BF16)
        z = _gelu_tanh(yt_ref[:, tk] + d_ref[...] * ut_ref[:, tk]).T.astype(BF16)
        ys5 = _dot(z, gv_ref[...].astype(BF16)) * jax.nn.sigmoid(_dot(z, gg_ref[...].astype(BF16)))
        gate_s5 = jax.nn.sigmoid(_dot_nt(h, wgate_ref[0, :d, :].astype(BF16)))
        mix = gate_s5 * _dot(ys5.astype(BF16), wbs_ref[...].astype(BF16))
        gate_ml = jax.nn.sigmoid(_dot_nt(h, wgate_ref[0, d:, :].astype(BF16)))
        mix = mix + gate_ml * _dot(yml_ref[tk, :].astype(BF16), wbm_ref[...].astype(BF16))
        o_ref[tk, :] = x + _dot(mix.astype(BF16), wout_ref[...].astype(BF16))


def _merge(x, yt, ut, yml, norm, w_in, gate_row, d_skip, gv, gg, wbs, wbm, wout, layer):
    n, d = x.shape
    T, s5w, nc = yt.shape
    mlw = yml.shape[1]

    def tok(w):
        return pl.BlockSpec((nc, w), lambda i: (i, 0))

    slab = pl.BlockSpec((None, s5w, nc), lambda i: (i, 0, 0))
    gate_rows = pl.BlockSpec((pl.Element(1), pl.Element(2 * d), pl.Element(d)),
                             lambda i: (layer, gate_row, 0), pipeline_mode=pl.Buffered(1))
    return pl.pallas_call(
        _merge_kernel,
        grid=(T,),
        in_specs=[tok(d), slab, slab, tok(mlw),
                  _const_spec((1, d), layer), gate_rows,
                  _const_spec((s5w, 1), layer), _const_spec((s5w, s5w), layer),
                  _const_spec((s5w, s5w), layer), _const_spec((s5w, d), layer),
                  _const_spec((mlw, d), layer), _const_spec((d, d), layer)],
        out_specs=tok(d),
        out_shape=jax.ShapeDtypeStruct((n, d), F32),
        compiler_params=_params(("parallel",)),
        name="merge",
    )(x, yt, ut, yml, norm, w_in, d_skip, gv, gg, wbs, wbm, wout)


def kernel(x, ffn1_norm, ffn1_wg, ffn1_wu, ffn1_wd, mix_norm, w_in, b_if, s5_lam_re, s5_lam_im, s5_log_dt, s5_b_re, s5_b_im, s5_c_re, s5_c_im, s5_d, s5_glu_v, s5_glu_g, ml_conv_w, ml_conv_b, ml_wq, ml_wk, ml_norm, ml_skip, w_br_s5, w_br_ml, w_out, ffn2_norm, ffn2_wg, ffn2_wu, ffn2_wd, final_norm):
    B, L, D = x.shape
    depth = w_in.shape[0]
    s5w = s5_d.shape[1]
    mlw = ml_norm.shape[1]
    H = ml_wq.shape[1]
    T = S5_CHUNK
    ncb = L // T
    nc = B * ncb
    assert ncb % S5_SEGMENTS == 0 and ncb % ML_CHUNK_S5 == 0 and (T * nc) % TOKEN_TILE == 0
    o3 = s5w + 3 * mlw
    o4 = o3 + 2 * H

    row = lambda a: a[:, None, :]
    ffn1 = (row(ffn1_norm), ffn1_wg, ffn1_wu, ffn1_wd)
    ffn2 = (row(ffn2_norm), ffn2_wg, ffn2_wu, ffn2_wd)
    b_ifp = row(jnp.pad(b_if, ((0, 0), (0, GATE_LANES - 2 * H))))
    w_in = jnp.swapaxes(w_in, 1, 2)
    mixn, conv_b, ml_n, ml_s = row(mix_norm), row(ml_conv_b), row(ml_norm), row(ml_skip)
    d_skip = s5_d[:, :, None]
    fin = final_norm[None, :]
    s5_params = _s5_pack(s5_lam_re, s5_lam_im, s5_log_dt, s5_b_re, s5_b_im, s5_c_re, s5_c_im)

    xt = x.reshape(B, ncb, T, D).transpose(2, 0, 1, 3).reshape(T * nc, D)
    seq4 = lambda a: a.reshape(T, B, ncb, a.shape[-1])
    for l in range(depth):
        xt = _ffn(xt, *ffn1, fin, l, False)
        ut, xml, v, o, ifg = _inproj(xt, mixn, w_in, b_ifp, l, nc, s5w, mlw, 2 * H)
        yt = _s5_scan(ut, *s5_params, l, B)
        yml = _mlstm(seq4(xml), seq4(v), seq4(o), seq4(ifg), ml_conv_w, conv_b, ml_wq, ml_wk, ml_n, ml_s, l)
        xt = _merge(xt, yt, ut, yml.reshape(T * nc, mlw), mixn, w_in, o4, d_skip, s5_glu_v, s5_glu_g,
                    w_br_s5, w_br_ml, w_out, l)
        xt = _ffn(xt, *ffn2, fin, l, l == depth - 1)
    return xt.reshape(T, B, ncb, D).transpose(1, 2, 0, 3).reshape(B, L, D)
```

```python
import functools
import math

import jax
import jax.numpy as jnp
from jax import lax
from jax.experimental import pallas as pl
from jax.experimental.pallas import tpu as pltpu

F32 = jnp.float32
BF16 = jnp.bfloat16

NORM_EPS = 1e-6
S5_GROUP = 16
S5_CHUNK = 32
S5_SEGMENTS = 8
S5_GROUPS_PER_STEP = 4
ML_HEADS = 4
ML_CHUNK_S5 = 8
LANES = 128
SUBLANES = 8
GATE_LANES = LANES
TOKEN_TILE = 512
FF_CHUNK = 256
MERGE_SPLIT = 2
VMEM_LIMIT = 56 * 1024 * 1024
HIGHEST = lax.Precision.HIGHEST


def _rms(x, g):
    return x * lax.rsqrt(jnp.mean(x * x, axis=-1, keepdims=True) + NORM_EPS) * g


def _dot(a, b):
    return jnp.dot(a, b, preferred_element_type=F32)


def _dot_nt(a, b):
    return lax.dot_general(a, b, (((1,), (1,)), ((), ())), preferred_element_type=F32)


def _dot_tn(a, b):
    return lax.dot_general(a, b, (((0,), (0,)), ((), ())), preferred_element_type=F32)


def _const_spec(shape, layer=None):
    nd = len(shape)
    if layer is None:
        return pl.BlockSpec(shape, lambda *_: (0,) * nd, pipeline_mode=pl.Buffered(1))
    return pl.BlockSpec((None,) + tuple(shape), lambda *_: (layer,) + (0,) * nd,
                        pipeline_mode=pl.Buffered(1))


def _params(sem):
    return pltpu.CompilerParams(dimension_semantics=sem, vmem_limit_bytes=VMEM_LIMIT)


def _ffn_kernel(x_ref, g_ref, wg_hbm, wu_hbm, wd_hbm, gf_ref, o_ref, wg_ref, wu_ref, wd_ref, sem, *io_scr,
                layer, final, natural_in, natural_out):
    d_ff = wg_ref.shape[1]
    n_chunks = d_ff // FF_CHUNK
    chunk = lambda c: slice(c * FF_CHUNK, (c + 1) * FF_CHUNK)
    i = pl.program_id(0)
    n_steps = pl.num_programs(0)
    slot = lax.rem(i, 2)
    io_scr = list(io_scr)
    if natural_in:
        xbuf, xsem = io_scr.pop(0), io_scr.pop(0)
        x_copy = lambda step, sl: pltpu.make_async_copy(x_ref.at[:, step, :], xbuf.at[sl], xsem.at[sl])
    if natural_out:
        obuf, osem = io_scr.pop(0), io_scr.pop(0)
        o_copy = lambda step, sl: pltpu.make_async_copy(obuf.at[sl], o_ref.at[:, step, :], osem.at[sl])

    def copies(c):
        return (pltpu.make_async_copy(wg_hbm.at[layer, :, chunk(c)], wg_ref.at[:, chunk(c)], sem.at[0, c]),
                pltpu.make_async_copy(wu_hbm.at[layer, :, chunk(c)], wu_ref.at[:, chunk(c)], sem.at[1, c]),
                pltpu.make_async_copy(wd_hbm.at[layer, chunk(c), :], wd_ref.at[chunk(c), :], sem.at[2, c]))

    def compute(first_step):
        x = xbuf[slot] if natural_in else x_ref[...]
        h = _rms(x, g_ref[...]).astype(BF16)
        acc = jnp.zeros(x.shape, F32)
        for c in range(n_chunks):
            if first_step:
                for cp in copies(c):
                    cp.wait()
            g = _dot(h, wg_ref[:, chunk(c)].astype(BF16))
            u = _dot(h, wu_ref[:, chunk(c)].astype(BF16))
            a = (g * jax.nn.sigmoid(g) * u).astype(BF16)
            acc = acc + _dot(a, wd_ref[chunk(c), :].astype(BF16))
        y = x + 0.5 * acc
        if final:
            y = _rms(y, gf_ref[...])
        if natural_out:
            obuf[slot] = y
        else:
            o_ref[...] = y

    if natural_in:
        @pl.when(i == 0)
        def _():
            x_copy(0, 0).start()

        @pl.when(i + 1 < n_steps)
        def _():
            x_copy(i + 1, 1 - slot).start()

        x_copy(i, slot).wait()
    if natural_out:
        @pl.when(i >= 2)
        def _():
            o_copy(i - 2, slot).wait()

    @pl.when(i == 0)
    def _():
        for c in range(n_chunks):
            for cp in copies(c):
                cp.start()
        compute(True)

    @pl.when(i != 0)
    def _():
        compute(False)

    if natural_out:
        o_copy(i, slot).start()

        @pl.when(i == n_steps - 1)
        def _():
            o_copy(i, slot).wait()

            @pl.when(i >= 1)
            def _():
                o_copy(i - 1, 1 - slot).wait()


def _ffn(x, norm, wg, wu, wd, final_norm, layer, final, tile=TOKEN_TILE, natural_in=False, natural_out=False):
    if natural_in:
        assert tile == x.shape[0]
        steps, d = x.shape[1:]
    else:
        steps, d = x.shape[0] // tile, x.shape[1]
    d_ff = wg.shape[2]
    tok = pl.BlockSpec((tile, d), lambda i: (i, 0))
    hbm = pl.BlockSpec(memory_space=pl.ANY)
    io_scratch = []
    for manual in (natural_in, natural_out):
        if manual:
            io_scratch += [pltpu.VMEM((2, tile, d), F32), pltpu.SemaphoreType.DMA((2,))]
    return pl.pallas_call(
        functools.partial(_ffn_kernel, layer=layer, final=final, natural_in=natural_in, natural_out=natural_out),
        grid=(steps,),
        in_specs=[hbm if natural_in else tok, _const_spec((1, d), layer), hbm, hbm, hbm, _const_spec((1, d))],
        out_specs=hbm if natural_out else tok,
        out_shape=jax.ShapeDtypeStruct((tile, steps, d) if natural_out else (tile * steps, d), F32),
        scratch_shapes=[pltpu.VMEM((d, d_ff), F32), pltpu.VMEM((d, d_ff), F32), pltpu.VMEM((d_ff, d), F32),
                        pltpu.SemaphoreType.DMA((3, d_ff // FF_CHUNK))] + io_scratch,
        compiler_params=_params(("arbitrary",)),
        name="ffn",
    )(x, norm, wg, wu, wd, final_norm)


def _fetch_once(copies, body):
    @pl.when(pl.program_id(0) == 0)
    def _():
        for cp in copies:
            cp.start()
        body(lambda k: copies[k].wait())

    @pl.when(pl.program_id(0) != 0)
    def _():
        body(lambda k: None)


def _inproj_kernel(x_ref, g_ref, w_hbm, bif_ref, ut_ref, xml_ref, v_ref, o_ref, if_ref, w_ref, sem, *,
                   layer, n_gates):
    c0 = ut_ref.shape[0]
    c1 = c0 + xml_ref.shape[1]
    c2 = c1 + v_ref.shape[1]
    c3 = c2 + o_ref.shape[1]
    gl = if_ref.shape[1]
    pieces = [slice(0, c0), slice(c0, c1), slice(c1, c2), slice(c2, c3), slice(c3, c3 + gl)]
    copies = [pltpu.make_async_copy(w_hbm.at[layer, rows, :], w_ref.at[rows, :], sem.at[k])
              for k, rows in enumerate(pieces)]

    def body(wait):
        def w(k):
            wait(k)
            return w_ref[pieces[k], :].astype(BF16)

        h = _rms(x_ref[...], g_ref[...]).astype(BF16)
        ut_ref[...] = _dot_nt(w(0), h)
        xml_ref[...] = _dot_nt(h, w(1))
        v_ref[...] = _dot_nt(h, w(2))
        o_ref[...] = _dot_nt(h, w(3))
        gates = _dot_nt(h, w(4))
        lane = lax.broadcasted_iota(jnp.int32, gates.shape, 1)
        if_ref[...] = jnp.where(lane < n_gates, gates, 0.0) + bif_ref[...]

    _fetch_once(copies, body)


def _inproj(x, norm, w_in, b_if, layer, nc, s5w, mlw, n_gates):
    n, d = x.shape
    used = s5w + 3 * mlw + GATE_LANES
    assert used <= w_in.shape[1] and used % 8 == 0

    def tok(w):
        return pl.BlockSpec((nc, w), lambda i: (i, 0))

    widths = (mlw, mlw, mlw, GATE_LANES)
    return pl.pallas_call(
        functools.partial(_inproj_kernel, layer=layer, n_gates=n_gates),
        grid=(n // nc,),
        in_specs=[tok(d), _const_spec((1, d), layer), pl.BlockSpec(memory_space=pl.ANY),
                  _const_spec((1, GATE_LANES), layer)],
        out_specs=[pl.BlockSpec((None, s5w, nc), lambda i: (i, 0, 0))] + [tok(w) for w in widths],
        out_shape=[jax.ShapeDtypeStruct((n // nc, s5w, nc), F32)]
        + [jax.ShapeDtypeStruct((n, w), F32) for w in widths],
        scratch_shapes=[pltpu.VMEM((used, d), F32), pltpu.SemaphoreType.DMA((5,))],
        compiler_params=_params(("arbitrary",)),
        name="inproj",
    )(x, norm, w_in, b_if)


def _s5_pack(lam_re, lam_im, log_dt, b_re, b_im, c_re, c_im):
    two = lambda a: jnp.concatenate([a, a], axis=-1)
    lam = jnp.stack([two(lam_re), two(lam_im), jnp.broadcast_to(log_dt[..., None], two(lam_re).shape)]
                    + [jnp.zeros_like(two(lam_re))] * 5, axis=-2)
    bmat = jnp.concatenate([jnp.swapaxes(b_re, -1, -2), jnp.swapaxes(b_im, -1, -2)], axis=-1)
    cc = jnp.concatenate([two(c_re), two(c_im)], axis=-2)
    return lam, bmat, cc


def _s5_operators(lam, b1t, cc, T, P, R):
    n2 = lam.shape[1]
    N = n2 // 2
    tp = T * P
    lr = jnp.minimum(lam[0:1, :], -1e-4)
    li = lam[1:2, :]
    dt = jnp.exp(lam[2:3, :])
    lrdt = lr * dt
    lidt = li * dt
    half = lax.broadcasted_iota(jnp.int32, (1, n2), 1) < N

    def powers(k):
        pm = jnp.exp(lrdt * k)
        return pm * jnp.cos(lidt * k), pm * jnp.sin(lidt * k)

    kk = lax.broadcasted_iota(jnp.int32, (T + 8, 1), 0).astype(F32)
    pre, pim = powers(kk)
    rr = lax.broadcasted_iota(jnp.int32, (R + 8, 1), 0).astype(F32) * float(T)
    sre, sim = powers(rr)

    def forms(re, im):
        f2 = jnp.where(half, -im, im)
        return re, f2, -f2

    def to_columns(rows):
        pad = jnp.zeros((n2 - rows.shape[0], n2), F32)
        return jnp.concatenate([rows, pad], axis=0).T

    ab_re, ab_im = pre[1:2, :], pim[1:2, :]
    den = lr * lr + li * li
    q_re = ((ab_re - 1.0) * lr + ab_im * li) / den
    q_im = (ab_im * lr - (ab_re - 1.0) * li) / den
    swapped = pltpu.roll(b1t, N, 1)
    b2t = jnp.where(half, -swapped, swapped)
    bbs_t = q_re * b1t + q_im * b2t

    cc1, cc2 = cc[0:P, :], cc[P:2 * P, :]
    pp1 = jnp.where(half, pre, -pim)
    pp2 = jnp.where(half, -pim, -pre)

    def readout(k0):
        return jnp.concatenate([pp1[k:k + 1, :] * cc1 + pp2[k:k + 1, :] * cc2 for k in range(k0, k0 + T)], axis=0)

    v = readout(1).astype(BF16)
    kcol_t = lax.dot_general(bbs_t, readout(0), (((1,), (1,)), ((), ())), preferred_element_type=F32,
                             precision=HIGHEST).astype(BF16)

    col_i = lax.broadcasted_iota(jnp.int32, (n2, tp), 1) // P
    pick_i = jnp.where(col_i == T - 1 - lax.broadcasted_iota(jnp.int32, (n2, tp), 0), 1.0, 0.0).astype(BF16)
    col_p = lax.broadcasted_iota(jnp.int32, (P, tp), 1) % P
    pick_p = jnp.where(col_p == lax.broadcasted_iota(jnp.int32, (P, tp), 0), 1.0, 0.0).astype(BF16)

    def split(a):
        hi = a.astype(BF16)
        return hi, (a - hi.astype(F32)).astype(BF16)

    def expand(a, e):
        hi, lo = split(a)
        return _dot(hi, e) + _dot(lo, e)

    b_hi, b_lo = split(bbs_t)
    bt1 = _dot_tn(b_hi, pick_p) + _dot_tn(b_lo, pick_p)
    bt2 = jnp.concatenate([-bt1[N:, :], bt1[:N, :]], axis=0)
    w_s = expand(to_columns(pre[0:T, :]), pick_i) * bt1 + expand(to_columns(pim[0:T, :]), pick_i) * bt2
    w = jnp.concatenate([w_s, w_s[N:, :], w_s[:N, :]], axis=0).astype(BF16)

    step = forms(pre[T:T + 1, :], pim[T:T + 1, :])
    seg = forms(sre[R:R + 1, :], sim[R:R + 1, :])
    pw1, pw2, _ = forms(sre[0:R, :], sim[0:R, :])
    return kcol_t, w, v, step, seg, pw1, pw2


def _toeplitz(kcol_t, T, P):
    tp = T * P
    lane = lax.broadcasted_iota(jnp.int32, (P, tp), 1)
    sub = lax.broadcasted_iota(jnp.int32, (P, tp), 0)
    rep = jnp.where(lane % P == sub, 1.0, 0.0).astype(BF16)
    m = _dot_tn(kcol_t, rep)
    blk = lax.broadcasted_iota(jnp.int32, (tp, tp), 1) // P
    shift = P
    while shift < tp:
        moved = jnp.concatenate([jnp.zeros((shift, tp), F32), m[:tp - shift, :]], axis=0)
        m = jnp.where((blk & (shift // P)) != 0, moved, m)
        shift *= 2
    return m.astype(BF16)


def _s5_kernel(u_ref, lam_ref, b_ref, cc_ref, y_ref, *, batch):
    P = S5_GROUP
    for j in range(lam_ref.shape[0]):
        rows = slice(j * P, (j + 1) * P)
        _s5_group(u_ref.at[:, rows, :], lam_ref[j], b_ref[j], cc_ref[j], y_ref.at[:, rows, :], batch)


def _s5_group(u_ref, lam, b1t, cc, y_ref, batch):
    T, P, nct = u_ref.shape
    n2 = lam.shape[1]
    S = S5_SEGMENTS
    ncb = nct // batch
    R = ncb // S
    kcol_t, w, v, (c1, c2, c2t), (g1, g2, g2t), pw1, pw2 = _s5_operators(lam, b1t, cc, T, P, R)
    u = u_ref[...].reshape(T * P, nct).astype(BF16)
    m = _toeplitz(kcol_t, T, P)
    x = _dot(w, u).T
    xr = [jnp.swapaxes(x[b * ncb:(b + 1) * ncb, :].reshape(S, R, 2 * n2), 0, 1).reshape(ncb, 2 * n2)
          for b in range(batch)]
    rows = lax.broadcasted_iota(jnp.int32, (S, n2), 0)

    s = [jnp.zeros((S, n2), F32)] * batch
    t = [jnp.zeros((S, n2), F32)] * batch
    local = [[] for _ in range(batch)]
    for r in range(R):
        for b in range(batch):
            local[b].append(s[b])
            xin = xr[b][r * S:(r + 1) * S, :]
            s[b], t[b] = c1 * s[b] + c2 * t[b] + xin[:, :n2], c1 * t[b] + c2t * s[b] + xin[:, n2:]
    ini_s = [jnp.zeros((S, n2), F32)] * batch
    ini_t = [jnp.zeros((S, n2), F32)] * batch
    cur_s = [jnp.zeros((1, n2), F32)] * batch
    cur_t = [jnp.zeros((1, n2), F32)] * batch
    for k in range(1, S):
        for b in range(batch):
            cur_s[b], cur_t[b] = (g1 * cur_s[b] + g2 * cur_t[b] + s[b][k - 1:k, :],
                                  g1 * cur_t[b] + g2t * cur_s[b] + t[b][k - 1:k, :])
            ini_s[b] = jnp.where(rows == k, cur_s[b], ini_s[b])
            ini_t[b] = jnp.where(rows == k, cur_t[b], ini_t[b])
    before = []
    for b in range(batch):
        sb = jnp.concatenate([local[b][r] + pw1[r:r + 1, :] * ini_s[b] + pw2[r:r + 1, :] * ini_t[b]
                              for r in range(R)], axis=0)
        before.append(jnp.swapaxes(sb.reshape(R, S, n2), 0, 1).reshape(ncb, n2))
    y = _dot(m, u) + _dot_nt(v, jnp.concatenate(before, axis=0).astype(BF16))
    y_ref[...] = y.reshape(T, P, nct)


def _s5_scan(ut, lam, bmat, cc, layer, batch):
    T, s5w, nct = ut.shape
    G = s5w // S5_GROUP
    gs = S5_GROUPS_PER_STEP

    def grp(a):
        return pl.BlockSpec((None, gs) + a.shape[2:], lambda g: (layer, g, 0, 0))

    slab = pl.BlockSpec((T, gs * S5_GROUP, nct), lambda g: (0, g, 0))
    return pl.pallas_call(
        functools.partial(_s5_kernel, batch=batch),
        grid=(G // gs,),
        in_specs=[slab, grp(lam), grp(bmat), grp(cc)],
        out_specs=slab,
        out_shape=jax.ShapeDtypeStruct((T, s5w, nct), F32),
        compiler_params=_params(("parallel",)),
        name="s5_scan",
    )(ut, lam, bmat, cc)


def _mlstm_kernel(xml_ref, v_ref, o_ref, if_ref, cw_ref, cb_ref, wq_ref, wk_ref, ng_ref, sk_ref,
                  y_ref, c_scr, n_scr, m_scr, tail_scr, tri_scr, neg_scr):
    H = wq_ref.shape[0]
    dh = wq_ref.shape[1]
    T, B, CS, W = xml_ref.shape
    L = T * CS
    K = cw_ref.shape[0]

    @pl.when(pl.program_id(0) == 0)
    def _():
        c_scr[...] = jnp.zeros(c_scr.shape, F32)
        n_scr[...] = jnp.zeros(n_scr.shape, F32)
        m_scr[...] = jnp.zeros(m_scr.shape, F32)
        tail_scr[...] = jnp.zeros(tail_scr.shape, F32)
        ri = lax.broadcasted_iota(jnp.int32, (L, L), 0)
        ci = lax.broadcasted_iota(jnp.int32, (L, L), 1)
        causal = ((ci % CS) * T + ci // CS) <= ((ri % CS) * T + ri // CS)
        tri_scr[...] = jnp.where(causal, 1.0, 0.0)
        neg_scr[...] = jnp.where(causal, 0.0, -jnp.inf)

    nw = (K - 1) * CS
    crow = lax.broadcasted_iota(jnp.int32, (nw, W), 0) % CS
    ones = jnp.ones((L, n_scr.shape[2]), BF16)
    for b, h in [(b, h) for b in range(B) for h in range(H)]:
        if h == 0:
            x = xml_ref[:, b, :, :].reshape(L, W)
            last = x[L - nw:, :]
            wrap = jnp.where(crow == 0, pltpu.roll(tail_scr[b], nw - (CS - 1), 0), pltpu.roll(last, 1, 0))
            tail_scr[b] = last
            conv = cb_ref[...] + cw_ref[K - 1:K, :] * x
            for d in range(1, K):
                xd = jnp.concatenate([wrap[nw - d * CS:, :], x[:L - d * CS, :]], axis=0)
                conv = conv + cw_ref[K - 1 - d:K - d, :] * xd
            xc = conv * jax.nn.sigmoid(conv)
            xcb = xc.astype(BF16)
            g = if_ref[:, b, :, :].reshape(L, if_ref.shape[3])
            lf = jnp.minimum(g, 0.0) - jnp.log1p(jnp.exp(-jnp.abs(g)))
            bcum = jnp.dot(tri_scr[...], lf, preferred_element_type=F32, precision=HIGHEST)
            bcum_t = bcum.T
            g_t = g.T
        st = b * H + h
        cs = slice(h * dh, (h + 1) * dh)
        q32 = _dot(xcb[:, cs], wq_ref[h].astype(BF16)) * (dh ** -0.5)
        q = q32.astype(BF16)
        kf = _dot(xcb[:, cs], wk_ref[h].astype(BF16))
        v1 = jnp.concatenate([v_ref[:, b, :, cs].reshape(L, dh).astype(BF16), ones], axis=1)
        b_col = bcum[:, H + h:H + h + 1]
        r_col = g[:, h:h + 1] - b_col
        r_row = g_t[h:h + 1, :] - bcum_t[H + h:H + h + 1, :]
        m_prev = m_scr[st, 0:1, 0:1]
        c_prev = c_scr[st]
        n_prev = n_scr[st]

        arg = r_row + neg_scr[...]
        mu = jnp.maximum(m_prev, jnp.max(arg, axis=-1, keepdims=True))
        w_inter = jnp.exp(m_prev - mu)
        s = _dot_nt(q, kf.astype(BF16)) * jnp.exp(arg - mu)
        lhs = jnp.concatenate([s.astype(BF16), (q32 * w_inter).astype(BF16)], axis=1)
        rhs = jnp.concatenate([v1, jnp.concatenate([c_prev.astype(BF16), n_prev.astype(BF16)], axis=1)], axis=0)
        nd = _dot(lhs, rhs)
        rden = 1.0 / jnp.maximum(jnp.abs(nd[:, dh:]), jnp.exp(-(b_col + mu)))
        ht = nd[:, :dh] * jnp.concatenate([rden] * (dh // rden.shape[1]), axis=1)

        mu_end = jnp.maximum(m_prev, jnp.max(r_row, axis=-1, keepdims=True))
        w_old = jnp.exp(m_prev - mu_end)
        kw = kf * jnp.exp(r_col - mu_end)
        upd = _dot_tn(kw.astype(BF16), v1)
        c_scr[st] = w_old * c_prev + upd[:, :dh]
        n_scr[st] = w_old * n_prev + upd[:, dh:]
        m_scr[st] = jnp.broadcast_to(bcum_t[H + h:H + h + 1, L - 1:L] + mu_end, m_scr.shape[1:])

        hc = jax.nn.sigmoid(o_ref[:, b, :, cs].reshape(L, dh)) * ht
        hn = hc * lax.rsqrt(jnp.mean(hc * hc, axis=-1, keepdims=True) + NORM_EPS)
        y_ref[:, b, :, cs] = (hn * ng_ref[:, cs] + sk_ref[:, cs] * xc[:, cs]).reshape(T, CS, dh)


def _mlstm(xml, v, o, ifg, conv_w, conv_b, wq, wk, norm_g, skip, layer):
    T, B, ncb, W = xml.shape
    H, dh = wq.shape[1], wq.shape[2]
    K = conv_w.shape[1]
    CS = ML_CHUNK_S5
    L = T * CS

    def seq(w):
        return pl.BlockSpec((T, B, CS, w), lambda c: (0, 0, c, 0))

    return pl.pallas_call(
        _mlstm_kernel,
        grid=(ncb // CS,),
        in_specs=[seq(W), seq(W), seq(W), seq(GATE_LANES),
                  _const_spec((K, W), layer), _const_spec((1, W), layer),
                  _const_spec((H, dh, dh), layer), _const_spec((H, dh, dh), layer),
                  _const_spec((1, W), layer), _const_spec((1, W), layer)],
        out_specs=seq(W),
        out_shape=jax.ShapeDtypeStruct((T, B, ncb, W), F32),
        scratch_shapes=[pltpu.VMEM((B * H, dh, dh), F32), pltpu.VMEM((B * H, dh, LANES), F32),
                        pltpu.VMEM((B * H, SUBLANES, LANES), F32), pltpu.VMEM((B, (K - 1) * CS, W), F32),
                        pltpu.VMEM((L, L), F32), pltpu.VMEM((L, L), F32)],
        compiler_params=_params(("arbitrary",)),
        name="mlstm",
    )(xml, v, o, ifg, conv_w, conv_b, wq, wk, norm_g, skip)


def _gelu_tanh(x):
    return 0.5 * x * (1.0 + jnp.tanh(math.sqrt(2.0 / math.pi) * (x + 0.044715 * (x * x * x))))


def _merge_kernel(x_ref, yt_ref, ut_ref, yml_ref, g_ref, win_hbm, d_ref, gv_hbm, gg_hbm, wbs_hbm, wbm_hbm,
                  wout_hbm, o_ref, gv_ref, gg_ref, wgs_ref, wbs_ref, wgm_ref, wbm_ref, wout_ref, sem, *,
                  layer, gate_row):
    d = x_ref.shape[1]
    part = x_ref.shape[0] // MERGE_SPLIT
    pairs = [(gv_hbm.at[layer], gv_ref), (gg_hbm.at[layer], gg_ref),
             (win_hbm.at[layer, gate_row:gate_row + d, :], wgs_ref), (wbs_hbm.at[layer], wbs_ref),
             (win_hbm.at[layer, gate_row + d:gate_row + 2 * d, :], wgm_ref), (wbm_hbm.at[layer], wbm_ref),
             (wout_hbm.at[layer], wout_ref)]
    copies = [pltpu.make_async_copy(src, dst, sem.at[k]) for k, (src, dst) in enumerate(pairs)]

    def body(wait):
        def w(k):
            wait(k)
            return pairs[k][1][...].astype(BF16)

        for t in range(MERGE_SPLIT):
            if t == 1:
                wait = lambda k: None
            tk = slice(t * part, (t + 1) * part)
            x = x_ref[tk, :]
            h = _rms(x, g_ref[...]).astype(BF16)
            z = _gelu_tanh(yt_ref[:, tk] + d_ref[...] * ut_ref[:, tk]).T.astype(BF16)
            ys5 = _dot(z, w(0)) * jax.nn.sigmoid(_dot(z, w(1)))
            gate_s5 = jax.nn.sigmoid(_dot_nt(h, w(2)))
            mix = gate_s5 * _dot(ys5.astype(BF16), w(3))
            gate_ml = jax.nn.sigmoid(_dot_nt(h, w(4)))
            mix = mix + gate_ml * _dot(yml_ref[tk, :].astype(BF16), w(5))
            o_ref[tk, :] = x + _dot(mix.astype(BF16), w(6))

    _fetch_once(copies, body)


def _merge(x, yt, ut, yml, norm, w_in, gate_row, d_skip, gv, gg, wbs, wbm, wout, layer):
    n, d = x.shape
    T, s5w, nc = yt.shape
    mlw = yml.shape[1]

    def tok(w):
        return pl.BlockSpec((nc, w), lambda i: (i, 0))

    slab = pl.BlockSpec((None, s5w, nc), lambda i: (i, 0, 0))
    hbm = pl.BlockSpec(memory_space=pl.ANY)
    return pl.pallas_call(
        functools.partial(_merge_kernel, layer=layer, gate_row=gate_row),
        grid=(T,),
        in_specs=[tok(d), slab, slab, tok(mlw), _const_spec((1, d), layer), hbm,
                  _const_spec((s5w, 1), layer), hbm, hbm, hbm, hbm, hbm],
        out_specs=tok(d),
        out_shape=jax.ShapeDtypeStruct((n, d), F32),
        scratch_shapes=[pltpu.VMEM((s5w, s5w), F32), pltpu.VMEM((s5w, s5w), F32), pltpu.VMEM((d, d), F32),
                        pltpu.VMEM((s5w, d), F32), pltpu.VMEM((d, d), F32), pltpu.VMEM((mlw, d), F32),
                        pltpu.VMEM((d, d), F32), pltpu.SemaphoreType.DMA((7,))],
        compiler_params=_params(("arbitrary",)),
        name="merge",
    )(x, yt, ut, yml, norm, w_in, d_skip, gv, gg, wbs, wbm, wout)


def kernel(x, ffn1_norm, ffn1_wg, ffn1_wu, ffn1_wd, mix_norm, w_in, b_if, s5_lam_re, s5_lam_im, s5_log_dt, s5_b_re, s5_b_im, s5_c_re, s5_c_im, s5_d, s5_glu_v, s5_glu_g, ml_conv_w, ml_conv_b, ml_wq, ml_wk, ml_norm, ml_skip, w_br_s5, w_br_ml, w_out, ffn2_norm, ffn2_wg, ffn2_wu, ffn2_wd, final_norm):
    B, L, D = x.shape
    depth = w_in.shape[0]
    s5w = s5_d.shape[1]
    mlw = ml_norm.shape[1]
    H = ml_wq.shape[1]
    T = S5_CHUNK
    ncb = L // T
    nc = B * ncb
    assert ncb % S5_SEGMENTS == 0 and ncb % ML_CHUNK_S5 == 0 and (T * nc) % TOKEN_TILE == 0
    o3 = s5w + 3 * mlw
    o4 = o3 + 2 * H

    row = lambda a: a[:, None, :]
    ffn1 = (row(ffn1_norm), ffn1_wg, ffn1_wu, ffn1_wd)
    ffn2 = (row(ffn2_norm), ffn2_wg, ffn2_wu, ffn2_wd)
    b_ifp = row(jnp.pad(b_if, ((0, 0), (0, GATE_LANES - 2 * H))))
    w_in = jnp.swapaxes(w_in, 1, 2)
    mixn, conv_b, ml_n, ml_s = row(mix_norm), row(ml_conv_b), row(ml_norm), row(ml_skip)
    d_skip = s5_d[:, :, None]
    fin = final_norm[None, :]
    s5_params = _s5_pack(s5_lam_re, s5_lam_im, s5_log_dt, s5_b_re, s5_b_im, s5_c_re, s5_c_im)

    xt = x.reshape(nc, T, D)
    seq4 = lambda a: a.reshape(T, B, ncb, a.shape[-1])
    for l in range(depth):
        first, last = l == 0, l == depth - 1
        xt = _ffn(xt, *ffn1, fin, l, False, tile=nc if first else TOKEN_TILE, natural_in=first)
        ut, xml, v, o, ifg = _inproj(xt, mixn, w_in, b_ifp, l, nc, s5w, mlw, 2 * H)
        yt = _s5_scan(ut, *s5_params, l, B)
        yml = _mlstm(seq4(xml), seq4(v), seq4(o), seq4(ifg), ml_conv_w, conv_b, ml_wq, ml_wk, ml_n, ml_s, l)
        xt = _merge(xt, yt, ut, yml.reshape(T * nc, mlw), mixn, w_in, o4, d_skip, s5_glu_v, s5_glu_g,
                    w_br_s5, w_br_ml, w_out, l)
        xt = _ffn(xt, *ffn2, fin, l, last, tile=nc if last else TOKEN_TILE, natural_out=last)
    return xt.reshape(B, L, D)
```

```python
import functools
import math

import jax
import jax.numpy as jnp
from jax import lax
from jax.experimental import pallas as pl
from jax.experimental.pallas import tpu as pltpu

F32 = jnp.float32
BF16 = jnp.bfloat16

NORM_EPS = 1e-6
S5_GROUP = 16
S5_CHUNK = 32
S5_SEGMENTS = 8
S5_GROUPS_PER_STEP = 4
ML_HEADS = 4
ML_CHUNK_S5 = 8
LANES = 128
SUBLANES = 8
GATE_LANES = LANES
TOKEN_TILE = 512
FF_CHUNK = 256
MERGE_SPLIT = 2
VMEM_LIMIT = 56 * 1024 * 1024
HIGHEST = lax.Precision.HIGHEST


def _rms(x, g):
    return x * lax.rsqrt(jnp.mean(x * x, axis=-1, keepdims=True) + NORM_EPS) * g


def _dot(a, b):
    return jnp.dot(a, b, preferred_element_type=F32)


def _dot_nt(a, b):
    return lax.dot_general(a, b, (((1,), (1,)), ((), ())), preferred_element_type=F32)


def _dot_tn(a, b):
    return lax.dot_general(a, b, (((0,), (0,)), ((), ())), preferred_element_type=F32)


def _const_spec(shape, layer=None):
    nd = len(shape)
    if layer is None:
        return pl.BlockSpec(shape, lambda *_: (0,) * nd, pipeline_mode=pl.Buffered(1))
    return pl.BlockSpec((None,) + tuple(shape), lambda *_: (layer,) + (0,) * nd,
                        pipeline_mode=pl.Buffered(1))


def _params(sem):
    return pltpu.CompilerParams(dimension_semantics=sem, vmem_limit_bytes=VMEM_LIMIT)


def _ffn_kernel(x_ref, g_ref, wg_hbm, wu_hbm, wd_hbm, gf_ref, o_ref, wg_ref, wu_ref, wd_ref, sem, *io_scr,
                layer, final, natural_in, natural_out):
    d_ff = wg_ref.shape[1]
    n_chunks = d_ff // FF_CHUNK
    chunk = lambda c: slice(c * FF_CHUNK, (c + 1) * FF_CHUNK)
    i = pl.program_id(0)
    n_steps = pl.num_programs(0)
    slot = lax.rem(i, 2)
    io_scr = list(io_scr)
    if natural_in:
        xbuf, xsem = io_scr.pop(0), io_scr.pop(0)
        x_copy = lambda step, sl: pltpu.make_async_copy(x_ref.at[:, step, :], xbuf.at[sl], xsem.at[sl])
    if natural_out:
        obuf, osem = io_scr.pop(0), io_scr.pop(0)
        o_copy = lambda step, sl: pltpu.make_async_copy(obuf.at[sl], o_ref.at[:, step, :], osem.at[sl])

    def copies(c):
        return (pltpu.make_async_copy(wg_hbm.at[layer, :, chunk(c)], wg_ref.at[:, chunk(c)], sem.at[0, c]),
                pltpu.make_async_copy(wu_hbm.at[layer, :, chunk(c)], wu_ref.at[:, chunk(c)], sem.at[1, c]),
                pltpu.make_async_copy(wd_hbm.at[layer, chunk(c), :], wd_ref.at[chunk(c), :], sem.at[2, c]))

    def compute(first_step):
        x = xbuf[slot] if natural_in else x_ref[...]
        h = _rms(x, g_ref[...]).astype(BF16)
        acc = jnp.zeros(x.shape, F32)
        for c in range(n_chunks):
            if first_step:
                for cp in copies(c):
                    cp.wait()
            g = _dot(h, wg_ref[:, chunk(c)].astype(BF16))
            u = _dot(h, wu_ref[:, chunk(c)].astype(BF16))
            a = (g * jax.nn.sigmoid(g) * u).astype(BF16)
            acc = acc + _dot(a, wd_ref[chunk(c), :].astype(BF16))
        y = x + 0.5 * acc
        if final:
            y = _rms(y, gf_ref[...])
        if natural_out:
            obuf[slot] = y
        else:
            o_ref[...] = y

    if natural_in:
        @pl.when(i == 0)
        def _():
            x_copy(0, 0).start()

        @pl.when(i + 1 < n_steps)
        def _():
            x_copy(i + 1, 1 - slot).start()

        x_copy(i, slot).wait()
    if natural_out:
        @pl.when(i >= 2)
        def _():
            o_copy(i - 2, slot).wait()

    @pl.when(i == 0)
    def _():
        for c in range(n_chunks):
            for cp in copies(c):
                cp.start()
        compute(True)

    @pl.when(i != 0)
    def _():
        compute(False)

    if natural_out:
        o_copy(i, slot).start()

        @pl.when(i == n_steps - 1)
        def _():
            o_copy(i, slot).wait()

            @pl.when(i >= 1)
            def _():
                o_copy(i - 1, 1 - slot).wait()


def _ffn(x, norm, wg, wu, wd, final_norm, layer, final, tile=TOKEN_TILE, natural_in=False, natural_out=False):
    if natural_in:
        assert tile == x.shape[0]
        steps, d = x.shape[1:]
    else:
        steps, d = x.shape[0] // tile, x.shape[1]
    d_ff = wg.shape[2]
    tok = pl.BlockSpec((tile, d), lambda i: (i, 0))
    hbm = pl.BlockSpec(memory_space=pl.ANY)
    io_scratch = []
    for manual in (natural_in, natural_out):
        if manual:
            io_scratch += [pltpu.VMEM((2, tile, d), F32), pltpu.SemaphoreType.DMA((2,))]
    return pl.pallas_call(
        functools.partial(_ffn_kernel, layer=layer, final=final, natural_in=natural_in, natural_out=natural_out),
        grid=(steps,),
        in_specs=[hbm if natural_in else tok, _const_spec((1, d), layer), hbm, hbm, hbm, _const_spec((1, d))],
        out_specs=hbm if natural_out else tok,
        out_shape=jax.ShapeDtypeStruct((tile, steps, d) if natural_out else (tile * steps, d), F32),
        scratch_shapes=[pltpu.VMEM((d, d_ff), F32), pltpu.VMEM((d, d_ff), F32), pltpu.VMEM((d_ff, d), F32),
                        pltpu.SemaphoreType.DMA((3, d_ff // FF_CHUNK))] + io_scratch,
        compiler_params=_params(("arbitrary",)),
        name="ffn",
    )(x, norm, wg, wu, wd, final_norm)


def _inproj_kernel(x_ref, g_ref, w_ref, bif_ref, ut_ref, xml_ref, v_ref, o_ref, if_ref, *, n_gates):
    h = _rms(x_ref[...], g_ref[...]).astype(BF16)
    c0 = ut_ref.shape[0]
    c1 = c0 + xml_ref.shape[1]
    c2 = c1 + v_ref.shape[1]
    c3 = c2 + o_ref.shape[1]
    gl = if_ref.shape[1]
    ut_ref[...] = _dot_nt(w_ref[:c0, :].astype(BF16), h)
    xml_ref[...] = _dot_nt(h, w_ref[c0:c1, :].astype(BF16))
    v_ref[...] = _dot_nt(h, w_ref[c1:c2, :].astype(BF16))
    o_ref[...] = _dot_nt(h, w_ref[c2:c3, :].astype(BF16))
    gates = _dot_nt(h, w_ref[c3:c3 + gl, :].astype(BF16))
    lane = lax.broadcasted_iota(jnp.int32, gates.shape, 1)
    if_ref[...] = jnp.where(lane < n_gates, gates, 0.0) + bif_ref[...]


def _inproj(x, norm, w_in, b_if, layer, nc, s5w, mlw, n_gates):
    n, d = x.shape
    used = s5w + 3 * mlw + GATE_LANES
    assert used <= w_in.shape[1] and used % 8 == 0

    def tok(w):
        return pl.BlockSpec((nc, w), lambda i: (i, 0))

    widths = (mlw, mlw, mlw, GATE_LANES)
    return pl.pallas_call(
        functools.partial(_inproj_kernel, n_gates=n_gates),
        grid=(n // nc,),
        in_specs=[tok(d), _const_spec((1, d), layer), _const_spec((used, d), layer),
                  _const_spec((1, GATE_LANES), layer)],
        out_specs=[pl.BlockSpec((None, s5w, nc), lambda i: (i, 0, 0))] + [tok(w) for w in widths],
        out_shape=[jax.ShapeDtypeStruct((n // nc, s5w, nc), F32)]
        + [jax.ShapeDtypeStruct((n, w), F32) for w in widths],
        compiler_params=_params(("parallel",)),
        name="inproj",
    )(x, norm, w_in, b_if)


def _s5_pack(lam_re, lam_im, log_dt, b_re, b_im, c_re, c_im):
    two = lambda a: jnp.concatenate([a, a], axis=-1)
    lam = jnp.stack([two(lam_re), two(lam_im), jnp.broadcast_to(log_dt[..., None], two(lam_re).shape)]
                    + [jnp.zeros_like(two(lam_re))] * 5, axis=-2)
    bmat = jnp.concatenate([jnp.swapaxes(b_re, -1, -2), jnp.swapaxes(b_im, -1, -2)], axis=-1)
    cc = jnp.concatenate([two(c_re), two(c_im)], axis=-2)
    return lam, bmat, cc


def _s5_operators(lam, b1t, cc, T, P, R):
    n2 = lam.shape[1]
    N = n2 // 2
    tp = T * P
    lr = jnp.minimum(lam[0:1, :], -1e-4)
    li = lam[1:2, :]
    dt = jnp.exp(lam[2:3, :])
    lrdt = lr * dt
    lidt = li * dt
    half = lax.broadcasted_iota(jnp.int32, (1, n2), 1) < N

    def powers(k):
        pm = jnp.exp(lrdt * k)
        return pm * jnp.cos(lidt * k), pm * jnp.sin(lidt * k)

    kk = lax.broadcasted_iota(jnp.int32, (T + 8, 1), 0).astype(F32)
    pre, pim = powers(kk)
    rr = lax.broadcasted_iota(jnp.int32, (R + 8, 1), 0).astype(F32) * float(T)
    sre, sim = powers(rr)

    def forms(re, im):
        f2 = jnp.where(half, -im, im)
        return re, f2, -f2

    def to_columns(rows):
        pad = jnp.zeros((n2 - rows.shape[0], n2), F32)
        return jnp.concatenate([rows, pad], axis=0).T

    ab_re, ab_im = pre[1:2, :], pim[1:2, :]
    den = lr * lr + li * li
    q_re = ((ab_re - 1.0) * lr + ab_im * li) / den
    q_im = (ab_im * lr - (ab_re - 1.0) * li) / den
    swapped = pltpu.roll(b1t, N, 1)
    b2t = jnp.where(half, -swapped, swapped)
    bbs_t = q_re * b1t + q_im * b2t

    cc1, cc2 = cc[0:P, :], cc[P:2 * P, :]
    pp1 = jnp.where(half, pre, -pim)
    pp2 = jnp.where(half, -pim, -pre)

    def readout(k0):
        return jnp.concatenate([pp1[k:k + 1, :] * cc1 + pp2[k:k + 1, :] * cc2 for k in range(k0, k0 + T)], axis=0)

    v = readout(1).astype(BF16)
    kcol_t = lax.dot_general(bbs_t, readout(0), (((1,), (1,)), ((), ())), preferred_element_type=F32,
                             precision=HIGHEST).astype(BF16)

    col_i = lax.broadcasted_iota(jnp.int32, (n2, tp), 1) // P
    pick_i = jnp.where(col_i == T - 1 - lax.broadcasted_iota(jnp.int32, (n2, tp), 0), 1.0, 0.0).astype(BF16)
    col_p = lax.broadcasted_iota(jnp.int32, (P, tp), 1) % P
    pick_p = jnp.where(col_p == lax.broadcasted_iota(jnp.int32, (P, tp), 0), 1.0, 0.0).astype(BF16)

    def split(a):
        hi = a.astype(BF16)
        return hi, (a - hi.astype(F32)).astype(BF16)

    def expand(a, e):
        hi, lo = split(a)
        return _dot(hi, e) + _dot(lo, e)

    b_hi, b_lo = split(bbs_t)
    bt1 = _dot_tn(b_hi, pick_p) + _dot_tn(b_lo, pick_p)
    bt2 = jnp.concatenate([-bt1[N:, :], bt1[:N, :]], axis=0)
    w_s = expand(to_columns(pre[0:T, :]), pick_i) * bt1 + expand(to_columns(pim[0:T, :]), pick_i) * bt2
    w = jnp.concatenate([w_s, w_s[N:, :], w_s[:N, :]], axis=0).astype(BF16)

    step = forms(pre[T:T + 1, :], pim[T:T + 1, :])
    seg = forms(sre[R:R + 1, :], sim[R:R + 1, :])
    pw1, pw2, _ = forms(sre[0:R, :], sim[0:R, :])
    return kcol_t, w, v, step, seg, pw1, pw2


def _toeplitz(kcol_t, T, P):
    tp = T * P
    lane = lax.broadcasted_iota(jnp.int32, (P, tp), 1)
    sub = lax.broadcasted_iota(jnp.int32, (P, tp), 0)
    rep = jnp.where(lane % P == sub, 1.0, 0.0).astype(BF16)
    m = _dot_tn(kcol_t, rep)
    blk = lax.broadcasted_iota(jnp.int32, (tp, tp), 1) // P
    shift = P
    while shift < tp:
        moved = jnp.concatenate([jnp.zeros((shift, tp), F32), m[:tp - shift, :]], axis=0)
        m = jnp.where((blk & (shift // P)) != 0, moved, m)
        shift *= 2
    return m.astype(BF16)


def _s5_kernel(u_ref, lam_ref, b_ref, cc_ref, y_ref, *, batch):
    P = S5_GROUP
    for j in range(lam_ref.shape[0]):
        rows = slice(j * P, (j + 1) * P)
        _s5_group(u_ref.at[:, rows, :], lam_ref[j], b_ref[j], cc_ref[j], y_ref.at[:, rows, :], batch)


def _s5_group(u_ref, lam, b1t, cc, y_ref, batch):
    T, P, nct = u_ref.shape
    n2 = lam.shape[1]
    S = S5_SEGMENTS
    ncb = nct // batch
    R = ncb // S
    kcol_t, w, v, (c1, c2, c2t), (g1, g2, g2t), pw1, pw2 = _s5_operators(lam, b1t, cc, T, P, R)
    u = u_ref[...].reshape(T * P, nct).astype(BF16)
    m = _toeplitz(kcol_t, T, P)
    x = _dot(w, u).T
    xr = [jnp.swapaxes(x[b * ncb:(b + 1) * ncb, :].reshape(S, R, 2 * n2), 0, 1).reshape(ncb, 2 * n2)
          for b in range(batch)]
    rows = lax.broadcasted_iota(jnp.int32, (S, n2), 0)

    s = [jnp.zeros((S, n2), F32)] * batch
    t = [jnp.zeros((S, n2), F32)] * batch
    local = [[] for _ in range(batch)]
    for r in range(R):
        for b in range(batch):
            local[b].append(s[b])
            xin = xr[b][r * S:(r + 1) * S, :]
            s[b], t[b] = c1 * s[b] + c2 * t[b] + xin[:, :n2], c1 * t[b] + c2t * s[b] + xin[:, n2:]
    ini_s = [jnp.zeros((S, n2), F32)] * batch
    ini_t = [jnp.zeros((S, n2), F32)] * batch
    cur_s = [jnp.zeros((1, n2), F32)] * batch
    cur_t = [jnp.zeros((1, n2), F32)] * batch
    for k in range(1, S):
        for b in range(batch):
            cur_s[b], cur_t[b] = (g1 * cur_s[b] + g2 * cur_t[b] + s[b][k - 1:k, :],
                                  g1 * cur_t[b] + g2t * cur_s[b] + t[b][k - 1:k, :])
            ini_s[b] = jnp.where(rows == k, cur_s[b], ini_s[b])
            ini_t[b] = jnp.where(rows == k, cur_t[b], ini_t[b])
    before = []
    for b in range(batch):
        sb = jnp.concatenate([local[b][r] + pw1[r:r + 1, :] * ini_s[b] + pw2[r:r + 1, :] * ini_t[b]
                              for r in range(R)], axis=0)
        before.append(jnp.swapaxes(sb.reshape(R, S, n2), 0, 1).reshape(ncb, n2))
    y = _dot(m, u) + _dot_nt(v, jnp.concatenate(before, axis=0).astype(BF16))
    y_ref[...] = y.reshape(T, P, nct)


def _s5_scan(ut, lam, bmat, cc, layer, batch):
    T, s5w, nct = ut.shape
    G = s5w // S5_GROUP
    gs = S5_GROUPS_PER_STEP

    def grp(a):
        return pl.BlockSpec((None, gs) + a.shape[2:], lambda g: (layer, g, 0, 0))

    slab = pl.BlockSpec((T, gs * S5_GROUP, nct), lambda g: (0, g, 0))
    return pl.pallas_call(
        functools.partial(_s5_kernel, batch=batch),
        grid=(G // gs,),
        in_specs=[slab, grp(lam), grp(bmat), grp(cc)],
        out_specs=slab,
        out_shape=jax.ShapeDtypeStruct((T, s5w, nct), F32),
        compiler_params=_params(("parallel",)),
        name="s5_scan",
    )(ut, lam, bmat, cc)


def _mlstm_kernel(xml_ref, v_ref, if_ref, cw_ref, cb_ref, wq_ref, wk_ref,
                  y_ref, xc_ref, c_scr, n_scr, m_scr, tail_scr, tri_scr, neg_scr):
    H = wq_ref.shape[0]
    dh = wq_ref.shape[1]
    T, B, CS, W = xml_ref.shape
    L = T * CS
    K = cw_ref.shape[0]

    @pl.when(pl.program_id(0) == 0)
    def _():
        c_scr[...] = jnp.zeros(c_scr.shape, F32)
        n_scr[...] = jnp.zeros(n_scr.shape, F32)
        m_scr[...] = jnp.zeros(m_scr.shape, F32)
        tail_scr[...] = jnp.zeros(tail_scr.shape, F32)
        ri = lax.broadcasted_iota(jnp.int32, (L, L), 0)
        ci = lax.broadcasted_iota(jnp.int32, (L, L), 1)
        causal = ((ci % CS) * T + ci // CS) <= ((ri % CS) * T + ri // CS)
        tri_scr[...] = jnp.where(causal, 1.0, 0.0)
        neg_scr[...] = jnp.where(causal, 0.0, -jnp.inf)

    nw = (K - 1) * CS
    crow = lax.broadcasted_iota(jnp.int32, (nw, W), 0) % CS
    ones = jnp.ones((L, n_scr.shape[2]), BF16)
    for b, h in [(b, h) for b in range(B) for h in range(H)]:
        if h == 0:
            x = xml_ref[:, b, :, :].reshape(L, W)
            last = x[L - nw:, :]
            wrap = jnp.where(crow == 0, pltpu.roll(tail_scr[b], nw - (CS - 1), 0), pltpu.roll(last, 1, 0))
            tail_scr[b] = last
            conv = cb_ref[...] + cw_ref[K - 1:K, :] * x
            for d in range(1, K):
                xd = jnp.concatenate([wrap[nw - d * CS:, :], x[:L - d * CS, :]], axis=0)
                conv = conv + cw_ref[K - 1 - d:K - d, :] * xd
            xc = conv * jax.nn.sigmoid(conv)
            xcb = xc.astype(BF16)
            xc_ref[:, b, :, :] = xc.reshape(T, CS, W)
            g = if_ref[:, b, :, :].reshape(L, if_ref.shape[3])
            lf = jnp.minimum(g, 0.0) - jnp.log1p(jnp.exp(-jnp.abs(g)))
            bcum = jnp.dot(tri_scr[...], lf, preferred_element_type=F32, precision=HIGHEST)
            bcum_t = bcum.T
            g_t = g.T
        st = b * H + h
        cs = slice(h * dh, (h + 1) * dh)
        q32 = _dot(xcb[:, cs], wq_ref[h].astype(BF16)) * (dh ** -0.5)
        q = q32.astype(BF16)
        kf = _dot(xcb[:, cs], wk_ref[h].astype(BF16))
        v1 = jnp.concatenate([v_ref[:, b, :, cs].reshape(L, dh).astype(BF16), ones], axis=1)
        b_col = bcum[:, H + h:H + h + 1]
        r_col = g[:, h:h + 1] - b_col
        r_row = g_t[h:h + 1, :] - bcum_t[H + h:H + h + 1, :]
        m_prev = m_scr[st, 0:1, 0:1]
        c_prev = c_scr[st]
        n_prev = n_scr[st]

        arg = r_row + neg_scr[...]
        mu = jnp.maximum(m_prev, jnp.max(arg, axis=-1, keepdims=True))
        w_inter = jnp.exp(m_prev - mu)
        s = _dot_nt(q, kf.astype(BF16)) * jnp.exp(arg - mu)
        lhs = jnp.concatenate([s.astype(BF16), (q32 * w_inter).astype(BF16)], axis=1)
        rhs = jnp.concatenate([v1, jnp.concatenate([c_prev.astype(BF16), n_prev.astype(BF16)], axis=1)], axis=0)
        nd = _dot(lhs, rhs)
        rden = 1.0 / jnp.maximum(jnp.abs(nd[:, dh:]), jnp.exp(-(b_col + mu)))
        ht = nd[:, :dh] * jnp.concatenate([rden] * (dh // rden.shape[1]), axis=1)

        mu_end = jnp.maximum(m_prev, jnp.max(r_row, axis=-1, keepdims=True))
        w_old = jnp.exp(m_prev - mu_end)
        kw = kf * jnp.exp(r_col - mu_end)
        upd = _dot_tn(kw.astype(BF16), v1)
        c_scr[st] = w_old * c_prev + upd[:, :dh]
        n_scr[st] = w_old * n_prev + upd[:, dh:]
        m_scr[st] = jnp.broadcast_to(bcum_t[H + h:H + h + 1, L - 1:L] + mu_end, m_scr.shape[1:])

        y_ref[:, b, :, cs] = ht.reshape(T, CS, dh)


def _mlstm(xml, v, ifg, conv_w, conv_b, wq, wk, layer):
    T, B, ncb, W = xml.shape
    H, dh = wq.shape[1], wq.shape[2]
    K = conv_w.shape[1]
    CS = ML_CHUNK_S5
    L = T * CS

    def seq(w):
        return pl.BlockSpec((T, B, CS, w), lambda c: (0, 0, c, 0))

    return pl.pallas_call(
        _mlstm_kernel,
        grid=(ncb // CS,),
        in_specs=[seq(W), seq(W), seq(GATE_LANES),
                  _const_spec((K, W), layer), _const_spec((1, W), layer),
                  _const_spec((H, dh, dh), layer), _const_spec((H, dh, dh), layer)],
        out_specs=[seq(W), seq(W)],
        out_shape=[jax.ShapeDtypeStruct((T, B, ncb, W), F32)] * 2,
        scratch_shapes=[pltpu.VMEM((B * H, dh, dh), F32), pltpu.VMEM((B * H, dh, LANES), F32),
                        pltpu.VMEM((B * H, SUBLANES, LANES), F32), pltpu.VMEM((B, (K - 1) * CS, W), F32),
                        pltpu.VMEM((L, L), F32), pltpu.VMEM((L, L), F32)],
        compiler_params=_params(("arbitrary",)),
        name="mlstm",
    )(xml, v, ifg, conv_w, conv_b, wq, wk)


def _gelu_tanh(x):
    return 0.5 * x * (1.0 + jnp.tanh(math.sqrt(2.0 / math.pi) * (x + 0.044715 * (x * x * x))))


def _merge_kernel(x_ref, yt_ref, ut_ref, ht_ref, og_ref, xc_ref, ng_ref, sk_ref, g_ref, wgate_ref, d_ref,
                  gv_ref, gg_ref, wbs_ref, wbm_ref, wout_ref, o_ref, *, heads):
    d = x_ref.shape[1]
    dh = ht_ref.shape[1] // heads
    part = x_ref.shape[0] // MERGE_SPLIT
    for k in range(MERGE_SPLIT):
        tk = slice(k * part, (k + 1) * part)
        x = x_ref[tk, :]
        h = _rms(x, g_ref[...]).astype(BF16)
        hc = jax.nn.sigmoid(og_ref[tk, :]) * ht_ref[tk, :]
        hn = jnp.concatenate(
            [hc[:, j * dh:(j + 1) * dh] * lax.rsqrt(
                jnp.mean(hc[:, j * dh:(j + 1) * dh] * hc[:, j * dh:(j + 1) * dh], axis=-1, keepdims=True) + NORM_EPS)
             for j in range(heads)], axis=1)
        yml = hn * ng_ref[...] + sk_ref[...] * xc_ref[tk, :]
        z = _gelu_tanh(yt_ref[:, tk] + d_ref[...] * ut_ref[:, tk]).T.astype(BF16)
        ys5 = _dot(z, gv_ref[...].astype(BF16)) * jax.nn.sigmoid(_dot(z, gg_ref[...].astype(BF16)))
        gate_s5 = jax.nn.sigmoid(_dot_nt(h, wgate_ref[0, :d, :].astype(BF16)))
        mix = gate_s5 * _dot(ys5.astype(BF16), wbs_ref[...].astype(BF16))
        gate_ml = jax.nn.sigmoid(_dot_nt(h, wgate_ref[0, d:, :].astype(BF16)))
        mix = mix + gate_ml * _dot(yml.astype(BF16), wbm_ref[...].astype(BF16))
        o_ref[tk, :] = x + _dot(mix.astype(BF16), wout_ref[...].astype(BF16))


def _merge(x, yt, ut, ht, og, xc, ml_norm, ml_skip, heads, norm, w_in, gate_row, d_skip, gv, gg, wbs, wbm, wout,
           layer):
    n, d = x.shape
    T, s5w, nc = yt.shape
    mlw = ht.shape[1]

    def tok(w):
        return pl.BlockSpec((nc, w), lambda i: (i, 0))

    slab = pl.BlockSpec((None, s5w, nc), lambda i: (i, 0, 0))
    gate_rows = pl.BlockSpec((pl.Element(1), pl.Element(2 * d), pl.Element(d)),
                             lambda i: (layer, gate_row, 0), pipeline_mode=pl.Buffered(1))
    return pl.pallas_call(
        functools.partial(_merge_kernel, heads=heads),
        grid=(T,),
        in_specs=[tok(d), slab, slab, tok(mlw), tok(mlw), tok(mlw),
                  _const_spec((1, mlw), layer), _const_spec((1, mlw), layer),
                  _const_spec((1, d), layer), gate_rows,
                  _const_spec((s5w, 1), layer), _const_spec((s5w, s5w), layer),
                  _const_spec((s5w, s5w), layer), _const_spec((s5w, d), layer),
                  _const_spec((mlw, d), layer), _const_spec((d, d), layer)],
        out_specs=tok(d),
        out_shape=jax.ShapeDtypeStruct((n, d), F32),
        compiler_params=_params(("parallel",)),
        name="merge",
    )(x, yt, ut, ht, og, xc, ml_norm, ml_skip, norm, w_in, d_skip, gv, gg, wbs, wbm, wout)


def kernel(x, ffn1_norm, ffn1_wg, ffn1_wu, ffn1_wd, mix_norm, w_in, b_if, s5_lam_re, s5_lam_im, s5_log_dt, s5_b_re, s5_b_im, s5_c_re, s5_c_im, s5_d, s5_glu_v, s5_glu_g, ml_conv_w, ml_conv_b, ml_wq, ml_wk, ml_norm, ml_skip, w_br_s5, w_br_ml, w_out, ffn2_norm, ffn2_wg, ffn2_wu, ffn2_wd, final_norm):
    B, L, D = x.shape
    depth = w_in.shape[0]
    s5w = s5_d.shape[1]
    mlw = ml_norm.shape[1]
    H = ml_wq.shape[1]
    T = S5_CHUNK
    ncb = L // T
    nc = B * ncb
    assert ncb % S5_SEGMENTS == 0 and ncb % ML_CHUNK_S5 == 0 and (T * nc) % TOKEN_TILE == 0
    o3 = s5w + 3 * mlw
    o4 = o3 + 2 * H

    row = lambda a: a[:, None, :]
    ffn1 = (row(ffn1_norm), ffn1_wg, ffn1_wu, ffn1_wd)
    ffn2 = (row(ffn2_norm), ffn2_wg, ffn2_wu, ffn2_wd)
    b_ifp = row(jnp.pad(b_if, ((0, 0), (0, GATE_LANES - 2 * H))))
    w_in = jnp.swapaxes(w_in, 1, 2)
    mixn, conv_b, ml_n, ml_s = row(mix_norm), row(ml_conv_b), row(ml_norm), row(ml_skip)
    d_skip = s5_d[:, :, None]
    fin = final_norm[None, :]
    s5_params = _s5_pack(s5_lam_re, s5_lam_im, s5_log_dt, s5_b_re, s5_b_im, s5_c_re, s5_c_im)

    xt = x.reshape(nc, T, D)
    seq4 = lambda a: a.reshape(T, B, ncb, a.shape[-1])
    for l in range(depth):
        first, last = l == 0, l == depth - 1
        xt = _ffn(xt, *ffn1, fin, l, False, tile=nc if first else TOKEN_TILE, natural_in=first)
        ut, xml, v, o, ifg = _inproj(xt, mixn, w_in, b_ifp, l, nc, s5w, mlw, 2 * H)
        yt = _s5_scan(ut, *s5_params, l, B)
        ht, xc = _mlstm(seq4(xml), seq4(v), seq4(ifg), ml_conv_w, conv_b, ml_wq, ml_wk, l)
        xt = _merge(xt, yt, ut, ht.reshape(T * nc, mlw), o, xc.reshape(T * nc, mlw), ml_n, ml_s, H, mixn, w_in, o4,
                    d_skip, s5_glu_v, s5_glu_g, w_br_s5, w_br_ml, w_out, l)
        xt = _ffn(xt, *ffn2, fin, l, last, tile=nc if last else TOKEN_TILE, natural_out=last)
    return xt.reshape(B, L, D)
```
